```python
import math
import jax, jax.numpy as jnp
from jax import lax
import numpy as np

D_MODEL = 1024
BATCH = 32
SEQ = 2048
DEPTH = 1
DEC_BATCH = 32
DEC_SEQ = 64
PAST_LEN = 4096

CHUNK = 64
N_META = 16
HEAD_DIM = 64
ATTN_W = D_MODEL // 2
N_Q_HEADS = ATTN_W // HEAD_DIM
N_KV_HEADS = 2
REP = N_Q_HEADS // N_KV_HEADS
KV_W = N_KV_HEADS * HEAD_DIM
WINDOW = 128
WIN_CHUNKS = WINDOW // CHUNK
SSM_W = D_MODEL - ATTN_W
SSM_CH = 16
SSM_G = SSM_W // SSM_CH
SSM_P = 64
D_FF = 4 * D_MODEL
IN_W = ATTN_W + 2 * KV_W + SSM_W
EPS = 1e-6
DT_MIN = 1e-3
DT_MAX = 1e-1
EIG_CLIP = -1e-4
NEG_INF = -1e30

kernel_name = "hymba_swa_sink_s5_streaming_step"


def rmsnorm(x, g):
    xf = x.astype(jnp.float32)
    y = xf * lax.rsqrt(jnp.mean(jnp.square(xf), axis=-1, keepdims=True) + EPS)
    return (y * g.astype(jnp.float32)).astype(x.dtype)


def project(h, p):
    z = rmsnorm(h, p["norm1_g"]) @ p["w_in"]
    q, k, v, u = jnp.split(z, [ATTN_W, ATTN_W + KV_W, ATTN_W + 2 * KV_W], axis=-1)
    lead = h.shape[:-1]
    q = rmsnorm(q.reshape(lead + (N_KV_HEADS, REP, HEAD_DIM)), p["q_norm_g"])
    k = rmsnorm(k.reshape(lead + (N_KV_HEADS, HEAD_DIM)), p["k_norm_g"])
    v = v.reshape(lead + (N_KV_HEADS, HEAD_DIM))
    u = u.reshape(lead + (SSM_G, SSM_CH))
    return q, k, v, u


def attend(q, k, v, sinks, mask):
    s = jnp.einsum('...qhrd,...khd->...hrqk', q, k).astype(jnp.float32) * (HEAD_DIM ** -0.5)
    if mask is not None:
        s = jnp.where(mask, s, NEG_INF)
    sink = jnp.broadcast_to(sinks.astype(jnp.float32).reshape(N_KV_HEADS, REP, 1, 1), s.shape[:-1] + (1,))
    pr = jax.nn.softmax(jnp.concatenate([s, sink], axis=-1), axis=-1)[..., :-1]
    return jnp.einsum('...hrqk,...khd->...qhrd', pr.astype(v.dtype), v)


def swa_prompt(q, k, v, sinks):
    b = q.shape[0]
    s_len = q.shape[1] - N_META
    nc = s_len // CHUNK
    out_meta = attend(q[:, :N_META], k[:, :N_META], v[:, :N_META], sinks, None)
    qr = q[:, N_META:].reshape(b, nc, CHUNK, N_KV_HEADS, REP, HEAD_DIM)

    def band(x):
        xp = jnp.pad(x[:, N_META:], ((0, 0), (WINDOW, 0), (0, 0), (0, 0)))
        xp = xp.reshape(b, nc + WIN_CHUNKS, CHUNK, N_KV_HEADS, HEAD_DIM)
        blocks = jnp.concatenate([xp[:, i:i + nc] for i in range(WIN_CHUNKS + 1)], axis=2)
        meta = jnp.broadcast_to(x[:, None, :N_META], (b, nc, N_META, N_KV_HEADS, HEAD_DIM))
        return jnp.concatenate([meta, blocks], axis=2)

    c = jnp.arange(nc)[:, None]
    j = jnp.arange((WIN_CHUNKS + 1) * CHUNK)[None, :]
    band_ok = (c - WIN_CHUNKS + j // CHUNK) >= 0
    mask = jnp.concatenate([jnp.ones((nc, N_META), dtype=bool), band_ok], axis=1)
    out_r = attend(qr, band(k), band(v), sinks, mask[:, None, None, None, :])
    out_r = out_r.reshape(b, s_len, N_KV_HEADS, REP, HEAD_DIM)
    return jnp.concatenate([out_meta, out_r], axis=1).reshape(b, N_META + s_len, ATTN_W)


def swa_sample(q, k, v, ck, cv, sinks):
    b, t = q.shape[0], q.shape[1]
    keys = jnp.concatenate([ck.astype(k.dtype), k], axis=1)
    vals = jnp.concatenate([cv.astype(v.dtype), v], axis=1)
    return attend(q, keys, vals, sinks, None).reshape(b, t, ATTN_W)


def s5_scan(u, h0_re, h0_im, p):
    uf = u.astype(jnp.float32)
    lam_re = jnp.minimum(p["ssm_A_re"].astype(jnp.float32), EIG_CLIP)
    lam_im = p["ssm_A_im"].astype(jnp.float32)
    dt = jnp.exp(p["ssm_log_dt"].astype(jnp.float32))[:, None]
    mag = jnp.exp(lam_re * dt)
    ar = mag * jnp.cos(lam_im * dt)
    ai = mag * jnp.sin(lam_im * dt)
    den = lam_re * lam_re + lam_im * lam_im
    cr = ((ar - 1.0) * lam_re + ai * lam_im) / den
    ci = (ai * lam_re - (ar - 1.0) * lam_im) / den
    b_re = p["ssm_B_re"].astype(jnp.float32)
    b_im = p["ssm_B_im"].astype(jnp.float32)
    bb_re = cr[..., None] * b_re - ci[..., None] * b_im
    bb_im = cr[..., None] * b_im + ci[..., None] * b_re
    bu_re = jnp.einsum('blgc,gpc->blgp', uf, bb_re)
    bu_im = jnp.einsum('blgc,gpc->blgp', uf, bb_im)
    if h0_re is not None:
        h0r = h0_re.astype(jnp.float32)
        h0i = h0_im.astype(jnp.float32)
        bu_re = bu_re.at[:, 0].add(ar * h0r - ai * h0i)
        bu_im = bu_im.at[:, 0].add(ar * h0i + ai * h0r)
    length = u.shape[1]
    a_re = jnp.broadcast_to(ar, (1, length, SSM_G, SSM_P))
    a_im = jnp.broadcast_to(ai, (1, length, SSM_G, SSM_P))

    def combine(e1, e2):
        a1r, a1i, b1r, b1i = e1
        a2r, a2i, b2r, b2i = e2
        return (a2r * a1r - a2i * a1i,
                a2r * a1i + a2i * a1r,
                a2r * b1r - a2i * b1i + b2r,
                a2r * b1i + a2i * b1r + b2i)

    _, _, xs_re, xs_im = lax.associative_scan(combine, (a_re, a_im, bu_re, bu_im), axis=1)
    y = (jnp.einsum('blgp,gcp->blgc', xs_re, p["ssm_C_re"].astype(jnp.float32))
         - jnp.einsum('blgp,gcp->blgc', xs_im, p["ssm_C_im"].astype(jnp.float32))
         + p["ssm_D"].astype(jnp.float32) * uf)
    return y, xs_re[:, -1], xs_im[:, -1]


def finish(h, attn, y_ssm, p):
    ys = y_ssm.reshape(y_ssm.shape[:-2] + (SSM_W,)).astype(h.dtype)
    zs = jax.nn.gelu(ys)
    s_out = zs * jax.nn.sigmoid(zs @ p["w_glu"] + p["b_glu"])
    merged = jnp.concatenate([rmsnorm(attn.astype(h.dtype), p["attn_out_g"]),
                              rmsnorm(s_out, p["ssm_out_g"])], axis=-1)
    h = h + merged @ p["w_out"]
    f = rmsnorm(h, p["norm2_g"]) @ p["w_up"]
    return h + jnp.square(jax.nn.relu(f)) @ p["w_down"]


def setup_inputs(seed: int = 0) -> dict:
    key = jax.random.key(seed)
    ks = jax.random.split(key, 32)
    f32 = jnp.float32
    nrm = lambda k, shape, scale: scale * jax.random.normal(k, shape, dtype=f32)
    a_im0 = jnp.pi * jnp.arange(SSM_P, dtype=f32)
    return {
        "x_prompt": nrm(ks[0], (BATCH, SEQ, D_MODEL), 1.0),
        "x_sample": nrm(ks[1], (DEC_BATCH, DEC_SEQ, D_MODEL), 1.0),
        "cache_swa_k": nrm(ks[2], (DEPTH, DEC_BATCH, N_META + WINDOW, N_KV_HEADS, HEAD_DIM), 1.0),
        "cache_swa_v": nrm(ks[3], (DEPTH, DEC_BATCH, N_META + WINDOW, N_KV_HEADS, HEAD_DIM), 1.0),
        "state_ssm_re": nrm(ks[4], (DEPTH, DEC_BATCH, SSM_G, SSM_P), 0.1),
        "state_ssm_im": nrm(ks[5], (DEPTH, DEC_BATCH, SSM_G, SSM_P), 0.1),
        "meta_tokens": nrm(ks[6], (N_META, D_MODEL), 1.0),
        "norm1_g": 1.0 + nrm(ks[7], (DEPTH, D_MODEL), 0.01),
        "w_in": nrm(ks[8], (DEPTH, D_MODEL, IN_W), D_MODEL ** -0.5),
        "q_norm_g": 1.0 + nrm(ks[9], (DEPTH, HEAD_DIM), 0.01),
        "k_norm_g": 1.0 + nrm(ks[10], (DEPTH, HEAD_DIM), 0.01),
        "sinks": nrm(ks[11], (DEPTH, N_Q_HEADS), 0.5),
        "ssm_A_re": -0.5 + nrm(ks[12], (DEPTH, SSM_G, SSM_P), 0.01),
        "ssm_A_im": a_im0 + nrm(ks[13], (DEPTH, SSM_G, SSM_P), 0.01),
        "ssm_log_dt": jax.random.uniform(ks[14], (DEPTH, SSM_G), dtype=f32,
                                         minval=math.log(DT_MIN), maxval=math.log(DT_MAX)),
        "ssm_B_re": nrm(ks[15], (DEPTH, SSM_G, SSM_P, SSM_CH), (2 * SSM_CH) ** -0.5),
        "ssm_B_im": nrm(ks[16], (DEPTH, SSM_G, SSM_P, SSM_CH), (2 * SSM_CH) ** -0.5),
        "ssm_C_re": nrm(ks[17], (DEPTH, SSM_G, SSM_CH, SSM_P), SSM_P ** -0.5),
        "ssm_C_im": nrm(ks[18], (DEPTH, SSM_G, SSM_CH, SSM_P), SSM_P ** -0.5),
        "ssm_D": nrm(ks[19], (DEPTH, SSM_G, SSM_CH), 1.0),
        "w_glu": nrm(ks[20], (DEPTH, SSM_W, SSM_W), SSM_W ** -0.5),
        "b_glu": nrm(ks[21], (DEPTH, SSM_W), 0.01),
        "attn_out_g": 1.0 + nrm(ks[22], (DEPTH, ATTN_W), 0.01),
        "ssm_out_g": 1.0 + nrm(ks[23], (DEPTH, SSM_W), 0.01),
        "w_out": nrm(ks[24], (DEPTH, D_MODEL, D_MODEL), D_MODEL ** -0.5),
        "norm2_g": 1.0 + nrm(ks[25], (DEPTH, D_MODEL), 0.01),
        "w_up": nrm(ks[26], (DEPTH, D_MODEL, D_FF), D_MODEL ** -0.5),
        "w_down": nrm(ks[27], (DEPTH, D_FF, D_MODEL), D_FF ** -0.5),
    }


def reference(x_prompt, x_sample, cache_swa_k, cache_swa_v, state_ssm_re, state_ssm_im,
              meta_tokens, norm1_g, w_in, q_norm_g, k_norm_g, sinks,
              ssm_A_re, ssm_A_im, ssm_log_dt, ssm_B_re, ssm_B_im, ssm_C_re, ssm_C_im, ssm_D,
              w_glu, b_glu, attn_out_g, ssm_out_g, w_out, norm2_g, w_up, w_down):
    b = x_prompt.shape[0]
    hp = jnp.concatenate([jnp.broadcast_to(meta_tokens.astype(x_prompt.dtype)[None], (b, N_META, D_MODEL)),
                          x_prompt], axis=1)
    hs = x_sample
    kp_l, vp_l, srp_l, sip_l, ks_l, vs_l, srs_l, sis_l = [], [], [], [], [], [], [], []
    for l in range(DEPTH):
        p = {"norm1_g": norm1_g[l], "w_in": w_in[l], "q_norm_g": q_norm_g[l], "k_norm_g": k_norm_g[l],
             "ssm_A_re": ssm_A_re[l], "ssm_A_im": ssm_A_im[l], "ssm_log_dt": ssm_log_dt[l],
             "ssm_B_re": ssm_B_re[l], "ssm_B_im": ssm_B_im[l], "ssm_C_re": ssm_C_re[l],
             "ssm_C_im": ssm_C_im[l], "ssm_D": ssm_D[l], "w_glu": w_glu[l], "b_glu": b_glu[l],
             "attn_out_g": attn_out_g[l], "ssm_out_g": ssm_out_g[l], "w_out": w_out[l],
             "norm2_g": norm2_g[l], "w_up": w_up[l], "w_down": w_down[l]}
        q, k, v, u = project(hp, p)
        attn = swa_prompt(q, k, v, sinks[l])
        y_ssm, fr, fi = s5_scan(u, None, None, p)
        kp_l.append(jnp.concatenate([k[:, :N_META], k[:, -WINDOW:]], axis=1))
        vp_l.append(jnp.concatenate([v[:, :N_META], v[:, -WINDOW:]], axis=1))
        srp_l.append(fr)
        sip_l.append(fi)
        hp = finish(hp, attn, y_ssm, p)
        q, k, v, u = project(hs, p)
        attn = swa_sample(q, k, v, cache_swa_k[l], cache_swa_v[l], sinks[l])
        y_ssm, fr, fi = s5_scan(u, state_ssm_re[l], state_ssm_im[l], p)
        ks_l.append(k)
        vs_l.append(v)
        srs_l.append(fr)
        sis_l.append(fi)
        hs = finish(hs, attn, y_ssm, p)
    y_prompt = hp[:, N_META:]
    y_sample = hs
    return (y_prompt, y_sample,
            jnp.stack(kp_l, 0), jnp.stack(vp_l, 0), jnp.stack(srp_l, 0), jnp.stack(sip_l, 0),
            jnp.stack(ks_l, 0), jnp.stack(vs_l, 0), jnp.stack(srs_l, 0), jnp.stack(sis_l, 0))
```

```python
import functools

import jax
import jax.numpy as jnp
from jax import lax
from jax.experimental import pallas as pl
from jax.experimental.pallas import tpu as pltpu

D_MODEL = 1024
CHUNK = 64
N_META = 16
HEAD_DIM = 64
ATTN_W = 512
N_KV_HEADS = 2
REP = 4
N_Q_HEADS = N_KV_HEADS * REP
KV_W = N_KV_HEADS * HEAD_DIM
WINDOW = 128
SSM_W = 512
SSM_CH = 16
SSM_G = 32
SSM_P = 64
D_FF = 4 * D_MODEL
IN_W = ATTN_W + 2 * KV_W + SSM_W
EPS = 1e-6
DT_MIN = 1e-3
DT_MAX = 1e-1
EIG_CLIP = -1e-4
NEG_INF = -1e30

SUBLANES = 8
LANES = 128
VMEM_LIMIT_BYTES = 56 * 1024 * 1024

BG = SUBLANES
ROWS = BG * CHUNK
QK_W = ATTN_W + KV_W
V_OFF = QK_W
U_OFF = QK_W + KV_W
N_KEYS = N_META + WINDOW + CHUNK
N_SLOTS = WINDOW // CHUNK + 1
HALF_G = SSM_G // 2
HALF_RI = HALF_G * SSM_P
HALF_STATE = 2 * HALF_RI
STATE_W = 2 * HALF_STATE
HALF_U = HALF_G * SSM_CH
U_SLABS = SSM_W // LANES
SCAN_W = 512
MLP_ROWS = 512
FF_BLK = 1024

BF16 = jnp.bfloat16
F32 = jnp.float32


def _rms(x, g):
    ms = jnp.mean(x * x, axis=-1, keepdims=True)
    return x * lax.rsqrt(ms + EPS) * g


def _project(x, g1_ref, win_ref, e_ref, gqk_ref):
    xn = _rms(x, g1_ref[...]).astype(BF16)
    z = jnp.dot(xn, win_ref[...], preferred_element_type=F32)
    zqk = z[:, :QK_W]
    ms = jnp.dot((zqk * zqk).astype(BF16), e_ref[...], preferred_element_type=F32)
    qk = zqk * lax.rsqrt(ms + EPS) * gqk_ref[...]
    return qk, z


def _meta_kernel(meta_ref, g1_ref, win_ref, e_ref, gqk_ref, a_ref, bbd_ref,
                 k_ref, v_ref, st_ref, bu_scr):
    qk, z = _project(meta_ref[...], g1_ref, win_ref, e_ref, gqk_ref)
    k_ref[...] = qk[:, ATTN_W:]
    v_ref[...] = z[:, V_OFF:U_OFF]
    u = z[:, U_OFF:].astype(BF16)
    for h in range(2):
        bu_scr[:, h * HALF_STATE:(h + 1) * HALF_STATE] = jnp.dot(
            u[:, h * HALF_U:(h + 1) * HALF_U], bbd_ref[h], preferred_element_type=F32)
    for h in range(2):
        ar = a_ref[2 * h:2 * h + 1, :]
        ai = a_ref[2 * h + 1:2 * h + 2, :]
        cr = h * HALF_STATE
        ci = cr + HALF_RI
        xr = jnp.zeros((1, HALF_RI), F32)
        xi = jnp.zeros((1, HALF_RI), F32)
        for t in range(N_META):
            br = bu_scr[t:t + 1, cr:cr + HALF_RI]
            bi = bu_scr[t:t + 1, ci:ci + HALF_RI]
            xr, xi = ar * xr - ai * xi + br, ar * xi + ai * xr + bi
        st_ref[:, cr:cr + HALF_RI] = xr
        st_ref[:, ci:ci + HALF_RI] = xi


def _mixer_kernel(x_ref, st0_ref, kmeta_ref, vmeta_ref, kprev_ref, vprev_ref,
                  g1_ref, win_ref, e_ref, gqk_ref, sinks_ref, a_ref, bbd_ref, ccat_ref, dvec_ref,
                  wglu_ref, bglu_ref, ga_ref, gs_ref, wout_ref,
                  h1_ref, kout_ref, vout_ref, st_ref,
                  qk_scr, attn_scr, utb_scr, bu_scr, ytb_scr, ybt_scr, kwin, vwin,
                  *, n_chunks, chunk_offset, n_out):
    c = pl.program_id(1)
    gc = c + chunk_offset

    @pl.when(c == 0)
    def _():
        st_ref[...] = st0_ref[...]
        kwin[:, 0:N_META, :] = kmeta_ref[...].astype(BF16)
        vwin[:, 0:N_META, :] = vmeta_ref[...].astype(BF16)
        kwin[:, N_META:N_META + WINDOW, :] = kprev_ref[...].astype(BF16)
        vwin[:, N_META:N_META + WINDOW, :] = vprev_ref[...].astype(BF16)
        kwin[:, N_META + WINDOW:, :] = jnp.zeros((BG, CHUNK, KV_W), BF16)
        vwin[:, N_META + WINDOW:, :] = jnp.zeros((BG, CHUNK, KV_W), BF16)

    x = x_ref[...].reshape(ROWS, D_MODEL)
    qk, z = _project(x, g1_ref, win_ref, e_ref, gqk_ref)
    qk_scr[...] = qk
    k = qk[:, ATTN_W:].reshape(BG, CHUNK, KV_W)
    v = z[:, V_OFF:U_OFF].reshape(BG, CHUNK, KV_W)
    u = z[:, U_OFF:]

    slot_row = pl.multiple_of(N_META + lax.rem(gc, N_SLOTS) * CHUNK, 16)
    kwin[:, pl.ds(slot_row, CHUNK), :] = k.astype(BF16)
    vwin[:, pl.ds(slot_row, CHUNK), :] = v.astype(BF16)
    for j in range(n_out):
        @pl.when(c == n_chunks - n_out + j)
        def _():
            kout_ref[:, j * CHUNK:(j + 1) * CHUNK, :] = k
            vout_ref[:, j * CHUNK:(j + 1) * CHUNK, :] = v

    key_limit = jnp.where(gc >= N_SLOTS - 1, N_KEYS, N_META + CHUNK * (gc + 1))
    key_ok = lax.broadcasted_iota(jnp.int32, (1, N_KEYS), 1) < key_limit
    lane = lax.broadcasted_iota(jnp.int32, (CHUNK, LANES), 1)
    head_lanes = (lane < HEAD_DIM, lane >= HEAD_DIM)

    def attn_body(b, carry):
        r0 = pl.multiple_of(b * CHUNK, CHUNK)
        qb = qk_scr[pl.ds(r0, CHUNK), 0:ATTN_W]
        parts = [jnp.where(head_lanes[kvh], qb[:, r * LANES:(r + 1) * LANES], 0.0)
                 for kvh in range(N_KV_HEADS) for r in range(REP)]
        qs = jnp.concatenate(parts, axis=0).astype(BF16)
        kb = kwin[b]
        vb = vwin[b]
        s = lax.dot_general(qs, kb, (((1,), (1,)), ((), ())), preferred_element_type=F32)
        s = jnp.where(key_ok, s, NEG_INF)
        p_parts = []
        inv_l = []
        for i in range(N_Q_HEADS):
            sb = s[i * CHUNK:(i + 1) * CHUNK]
            sink = sinks_ref[i]
            m = jnp.maximum(jnp.max(sb, axis=-1, keepdims=True), sink)
            p = jnp.exp(sb - m)
            l = jnp.sum(p, axis=-1, keepdims=True) + jnp.exp(sink - m)
            p_parts.append(p.astype(BF16))
            inv_l.append(1.0 / l)
        o = jnp.dot(jnp.concatenate(p_parts, axis=0), vb, preferred_element_type=F32)
        outs = []
        for r in range(REP):
            o0 = o[r * CHUNK:(r + 1) * CHUNK] * inv_l[r]
            o1 = o[(REP + r) * CHUNK:(REP + r + 1) * CHUNK] * inv_l[REP + r]
            outs.append(jnp.where(head_lanes[0], o0, o1))
        attn_scr[pl.ds(r0, CHUNK), :] = jnp.concatenate(outs, axis=1)
        return carry

    lax.fori_loop(0, BG, attn_body, 0)

    for b in range(BG):
        for j in range(U_SLABS):
            utb_scr[j, pl.ds(b, CHUNK, stride=BG), :] = (
                u[b * CHUNK:(b + 1) * CHUNK, j * LANES:(j + 1) * LANES])
    for h in range(2):
        uh = jnp.concatenate([utb_scr[2 * h], utb_scr[2 * h + 1]], axis=1).astype(BF16)
        bu_scr[:, h * HALF_STATE:(h + 1) * HALF_STATE] = jnp.dot(
            uh, bbd_ref[h], preferred_element_type=F32)

    for h in range(2):
        for w in range(HALF_RI // SCAN_W):
            cr = h * HALF_STATE + w * SCAN_W
            ci = cr + HALF_RI
            ar = jnp.broadcast_to(a_ref[2 * h:2 * h + 1, w * SCAN_W:(w + 1) * SCAN_W], (BG, SCAN_W))
            ai = jnp.broadcast_to(a_ref[2 * h + 1:2 * h + 2, w * SCAN_W:(w + 1) * SCAN_W], (BG, SCAN_W))

            def scan_body(t, carry, cr=cr, ci=ci, ar=ar, ai=ai):
                xr, xi = carry
                r = pl.multiple_of(t * BG, BG)
                br = bu_scr[pl.ds(r, BG), cr:cr + SCAN_W]
                bi = bu_scr[pl.ds(r, BG), ci:ci + SCAN_W]
                nr = ar * xr - ai * xi + br
                ni = ar * xi + ai * xr + bi
                bu_scr[pl.ds(r, BG), cr:cr + SCAN_W] = nr
                bu_scr[pl.ds(r, BG), ci:ci + SCAN_W] = ni
                return nr, ni

            xr, xi = lax.fori_loop(
                0, CHUNK, scan_body,
                (st_ref[:, cr:cr + SCAN_W], st_ref[:, ci:ci + SCAN_W]), unroll=8)
            st_ref[:, cr:cr + SCAN_W] = xr
            st_ref[:, ci:ci + SCAN_W] = xi

    for h in range(2):
        yh = jnp.dot(bu_scr[:, h * HALF_STATE:(h + 1) * HALF_STATE].astype(BF16), ccat_ref[h],
                     preferred_element_type=F32)
        for jj in range(2):
            ytb_scr[2 * h + jj] = yh[:, jj * LANES:(jj + 1) * LANES]
    for b in range(BG):
        for j in range(U_SLABS):
            ybt_scr[b * CHUNK:(b + 1) * CHUNK, j * LANES:(j + 1) * LANES] = (
                ytb_scr[j, pl.ds(b, CHUNK, stride=BG), :])

    ys = ybt_scr[...] + dvec_ref[...] * u
    zs = jax.nn.gelu(ys)
    gate = jnp.dot(zs.astype(BF16), wglu_ref[...], preferred_element_type=F32) + bglu_ref[...]
    s_out = zs * jax.nn.sigmoid(gate)
    merged = jnp.concatenate([_rms(attn_scr[...], ga_ref[...]), _rms(s_out, gs_ref[...])], axis=1)
    h1 = x + jnp.dot(merged.astype(BF16), wout_ref[...], preferred_element_type=F32)
    h1_ref[...] = h1.reshape(BG, CHUNK, D_MODEL)


def _mlp_kernel(h_ref, g2_ref, wup_ref, wdown_ref, o_ref):
    h = h_ref[...]
    xn = _rms(h, g2_ref[...]).astype(BF16)
    acc = h
    for j in range(D_FF // FF_BLK):
        f = jnp.dot(xn, wup_ref[:, j * FF_BLK:(j + 1) * FF_BLK], preferred_element_type=F32)
        f = jnp.maximum(f, 0.0)
        acc = acc + jnp.dot((f * f).astype(BF16), wdown_ref[j * FF_BLK:(j + 1) * FF_BLK, :],
                            preferred_element_type=F32)
    o_ref[...] = acc


def _const_spec(shape):
    n = len(shape)
    return pl.BlockSpec(shape, lambda *_: (0,) * n, pipeline_mode=pl.Buffered(1))


def _meta_call(meta, w):
    return pl.pallas_call(
        _meta_kernel,
        out_shape=(jax.ShapeDtypeStruct((N_META, KV_W), F32),
                   jax.ShapeDtypeStruct((N_META, KV_W), F32),
                   jax.ShapeDtypeStruct((1, STATE_W), F32)),
        scratch_shapes=[pltpu.VMEM((N_META, STATE_W), F32)],
        compiler_params=pltpu.CompilerParams(vmem_limit_bytes=VMEM_LIMIT_BYTES),
        name="meta",
    )(meta, w["g1"], w["w_in"], w["e"], w["gqk"], w["a"], w["bbd"])


def _mixer_call(x, st0, kmeta, vmeta, kprev, vprev, w, *, chunk_offset, n_out):
    nb, s_len, _ = x.shape
    n_chunks = s_len // CHUNK
    grid = (nb // BG, n_chunks)
    per_bg = lambda bg, c: (bg, 0, 0)
    in_specs = [
        pl.BlockSpec((BG, CHUNK, D_MODEL), lambda bg, c: (bg, c, 0)),
        pl.BlockSpec((BG, STATE_W), lambda bg, c: (bg, 0)),
        pl.BlockSpec((BG, N_META, KV_W), per_bg),
        pl.BlockSpec((BG, N_META, KV_W), per_bg),
        pl.BlockSpec((BG, WINDOW, KV_W), per_bg),
        pl.BlockSpec((BG, WINDOW, KV_W), per_bg),
        _const_spec((1, D_MODEL)),
        _const_spec((D_MODEL, IN_W)),
        _const_spec((QK_W, QK_W)),
        _const_spec((1, QK_W)),
        pl.BlockSpec(memory_space=pltpu.SMEM),
        _const_spec((4, HALF_RI)),
        _const_spec((2, HALF_U, HALF_STATE)),
        _const_spec((2, HALF_STATE, HALF_U)),
        _const_spec((1, SSM_W)),
        _const_spec((SSM_W, SSM_W)),
        _const_spec((1, SSM_W)),
        _const_spec((1, ATTN_W)),
        _const_spec((1, SSM_W)),
        _const_spec((D_MODEL, D_MODEL)),
    ]
    out_specs = (
        pl.BlockSpec((BG, CHUNK, D_MODEL), lambda bg, c: (bg, c, 0)),
        pl.BlockSpec((BG, n_out * CHUNK, KV_W), per_bg),
        pl.BlockSpec((BG, n_out * CHUNK, KV_W), per_bg),
        pl.BlockSpec((BG, STATE_W), lambda bg, c: (bg, 0)),
    )
    out_shape = (
        jax.ShapeDtypeStruct((nb, s_len, D_MODEL), F32),
        jax.ShapeDtypeStruct((nb, n_out * CHUNK, KV_W), F32),
        jax.ShapeDtypeStruct((nb, n_out * CHUNK, KV_W), F32),
        jax.ShapeDtypeStruct((nb, STATE_W), F32),
    )
    scratch = [
        pltpu.VMEM((ROWS, QK_W), F32),
        pltpu.VMEM((ROWS, ATTN_W), F32),
        pltpu.VMEM((U_SLABS, ROWS, LANES), F32),
        pltpu.VMEM((ROWS, STATE_W), F32),
        pltpu.VMEM((U_SLABS, ROWS, LANES), F32),
        pltpu.VMEM((ROWS, SSM_W), F32),
        pltpu.VMEM((BG, N_KEYS, KV_W), BF16),
        pltpu.VMEM((BG, N_KEYS, KV_W), BF16),
    ]
    kern = functools.partial(_mixer_kernel, n_chunks=n_chunks, chunk_offset=chunk_offset, n_out=n_out)
    return pl.pallas_call(
        kern, grid=grid, in_specs=in_specs, out_specs=out_specs, out_shape=out_shape,
        scratch_shapes=scratch,
        compiler_params=pltpu.CompilerParams(
            dimension_semantics=("arbitrary", "arbitrary"), vmem_limit_bytes=VMEM_LIMIT_BYTES),
        name="mixer",
    )(x, st0, kmeta, vmeta, kprev, vprev,
      w["g1"], w["w_in"], w["e"], w["gqk"], w["sinks"], w["a"], w["bbd"], w["ccat"], w["dvec"],
      w["w_glu"], w["b_glu"], w["ga"], w["gs"], w["w_out"])


def _mlp_call(h, w):
    n = h.shape[0]
    return pl.pallas_call(
        _mlp_kernel,
        grid=(n // MLP_ROWS,),
        in_specs=[pl.BlockSpec((MLP_ROWS, D_MODEL), lambda i: (i, 0)),
                  _const_spec((1, D_MODEL)),
                  _const_spec((D_MODEL, D_FF)),
                  _const_spec((D_FF, D_MODEL))],
        out_specs=pl.BlockSpec((MLP_ROWS, D_MODEL), lambda i: (i, 0)),
        out_shape=jax.ShapeDtypeStruct((n, D_MODEL), F32),
        compiler_params=pltpu.CompilerParams(
            dimension_semantics=("arbitrary",), vmem_limit_bytes=VMEM_LIMIT_BYTES),
        name="mlp",
    )(h, w["g2"], w["w_up"], w["w_down"])


def _prep_weights(norm1_g, w_in, q_norm_g, k_norm_g, sinks, ssm_A_re, ssm_A_im, ssm_log_dt,
                  ssm_B_re, ssm_B_im, ssm_C_re, ssm_C_im, ssm_D, w_glu, b_glu, attn_out_g,
                  ssm_out_g, w_out, norm2_g, w_up, w_down):
    def attn_perm(m):
        lead = m.shape[:-1]
        return m.reshape(lead + (N_KV_HEADS, REP, HEAD_DIM)).swapaxes(-3, -2).reshape(lead + (ATTN_W,))

    w_in_p = jnp.concatenate([attn_perm(w_in[:, :ATTN_W]), w_in[:, ATTN_W:]], axis=1).astype(BF16)
    e = jnp.kron(jnp.eye(QK_W // HEAD_DIM, dtype=F32),
                 jnp.full((HEAD_DIM, HEAD_DIM), 1.0 / HEAD_DIM, F32)).astype(BF16)
    gqk = jnp.concatenate([jnp.tile(q_norm_g, N_Q_HEADS) * (HEAD_DIM ** -0.5),
                           jnp.tile(k_norm_g, N_KV_HEADS)])[None, :]

    lam_re = jnp.minimum(ssm_A_re, EIG_CLIP)
    lam_im = ssm_A_im
    dt = jnp.exp(ssm_log_dt)[:, None]
    mag = jnp.exp(lam_re * dt)
    ar = mag * jnp.cos(lam_im * dt)
    ai = mag * jnp.sin(lam_im * dt)
    den = lam_re * lam_re + lam_im * lam_im
    cr = ((ar - 1.0) * lam_re + ai * lam_im) / den
    ci = (ai * lam_re - (ar - 1.0) * lam_im) / den
    bb_re = cr[..., None] * ssm_B_re - ci[..., None] * ssm_B_im
    bb_im = cr[..., None] * ssm_B_im + ci[..., None] * ssm_B_re
    a = jnp.stack([ar.reshape(2, HALF_RI), ai.reshape(2, HALF_RI)], axis=1).reshape(4, HALF_RI)
    eye = jnp.eye(HALF_G, dtype=F32)
    bb = jnp.stack([bb_re, bb_im], 0).reshape(2, 2, HALF_G, SSM_P, SSM_CH)
    bbd = jnp.einsum('ab,rhapc->hacrbp', eye, bb).reshape(2, HALF_U, HALF_STATE).astype(BF16)
    cc = jnp.stack([ssm_C_re, -ssm_C_im], 0).reshape(2, 2, HALF_G, SSM_CH, SSM_P)
    ccat = jnp.einsum('ab,rhacp->hrapbc', eye, cc).reshape(2, HALF_STATE, HALF_U).astype(BF16)

    w_out_p = jnp.concatenate(
        [attn_perm(w_out[:ATTN_W].T).T, w_out[ATTN_W:]], axis=0).astype(BF16)
    return {
        "g1": norm1_g[None, :], "w_in": w_in_p, "e": e, "gqk": gqk, "sinks": sinks,
        "a": a, "bbd": bbd, "ccat": ccat, "dvec": ssm_D.reshape(1, SSM_W),
        "w_glu": w_glu.astype(BF16), "b_glu": b_glu[None, :],
        "ga": attn_perm(attn_out_g)[None, :], "gs": ssm_out_g[None, :], "w_out": w_out_p,
        "g2": norm2_g[None, :], "w_up": w_up.astype(BF16), "w_down": w_down.astype(BF16),
    }


def _state_to_lanes(re, im):
    nb = re.shape[0]
    return jnp.stack([re.reshape(nb, 2, HALF_RI), im.reshape(nb, 2, HALF_RI)], axis=2).reshape(nb, STATE_W)


def _lanes_to_state(st):
    nb = st.shape[0]
    s = st.reshape(nb, 2, 2, HALF_RI)
    return s[:, :, 0].reshape(nb, SSM_G, SSM_P), s[:, :, 1].reshape(nb, SSM_G, SSM_P)


def kernel(x_prompt, x_sample, cache_swa_k, cache_swa_v, state_ssm_re, state_ssm_im, meta_tokens, norm1_g, w_in, q_norm_g, k_norm_g, sinks, ssm_A_re, ssm_A_im, ssm_log_dt, ssm_B_re, ssm_B_im, ssm_C_re, ssm_C_im, ssm_D, w_glu, b_glu, attn_out_g, ssm_out_g, w_out, norm2_g, w_up, w_down):
    assert norm1_g.shape[0] == 1, "single-layer trunk"
    nb, seq, _ = x_prompt.shape
    nd, dseq, _ = x_sample.shape
    assert dseq == CHUNK and seq % CHUNK == 0 and seq >= WINDOW and nb % BG == 0 and nd % BG == 0
    w = _prep_weights(norm1_g[0], w_in[0], q_norm_g[0], k_norm_g[0], sinks[0], ssm_A_re[0],
                      ssm_A_im[0], ssm_log_dt[0], ssm_B_re[0], ssm_B_im[0], ssm_C_re[0],
                      ssm_C_im[0], ssm_D[0], w_glu[0], b_glu[0], attn_out_g[0], ssm_out_g[0],
                      w_out[0], norm2_g[0], w_up[0], w_down[0])

    k_meta, v_meta, st_meta = _meta_call(meta_tokens, w)
    kmeta_b = jnp.broadcast_to(k_meta[None], (nb, N_META, KV_W))
    vmeta_b = jnp.broadcast_to(v_meta[None], (nb, N_META, KV_W))
    zeros_win = jnp.zeros((nb, WINDOW, KV_W), F32)
    h1p, kp, vp, stp = _mixer_call(
        x_prompt, jnp.broadcast_to(st_meta, (nb, STATE_W)), kmeta_b, vmeta_b, zeros_win, zeros_win,
        w, chunk_offset=0, n_out=WINDOW // CHUNK)
    y_prompt = _mlp_call(h1p.reshape(nb * seq, D_MODEL), w).reshape(nb, seq, D_MODEL)

    ck = cache_swa_k[0].reshape(nd, N_META + WINDOW, KV_W)
    cv = cache_swa_v[0].reshape(nd, N_META + WINDOW, KV_W)
    h1s, ks, vs, sts = _mixer_call(
        x_sample, _state_to_lanes(state_ssm_re[0], state_ssm_im[0]),
        ck[:, :N_META], cv[:, :N_META], ck[:, N_META:], cv[:, N_META:],
        w, chunk_offset=WINDOW // CHUNK, n_out=1)
    y_sample = _mlp_call(h1s.reshape(nd * dseq, D_MODEL), w).reshape(nd, dseq, D_MODEL)

    kv5 = lambda t: t.reshape(t.shape[0], t.shape[1], N_KV_HEADS, HEAD_DIM)[None]
    new_k_prompt = kv5(jnp.concatenate([kmeta_b, kp], axis=1))
    new_v_prompt = kv5(jnp.concatenate([vmeta_b, vp], axis=1))
    srp, sip = _lanes_to_state(stp)
    srs, sis = _lanes_to_state(sts)
    return (y_prompt, y_sample, new_k_prompt, new_v_prompt, srp[None], sip[None],
            kv5(ks), kv5(vs), srs[None], sis[None])
```

```python
import functools

import jax
import jax.numpy as jnp
from jax import lax
from jax.experimental import pallas as pl
from jax.experimental.pallas import tpu as pltpu

D_MODEL = 1024
CHUNK = 64
N_META = 16
HEAD_DIM = 64
ATTN_W = 512
N_KV_HEADS = 2
REP = 4
N_Q_HEADS = N_KV_HEADS * REP
KV_W = N_KV_HEADS * HEAD_DIM
WINDOW = 128
SSM_W = 512
SSM_CH = 16
SSM_G = 32
SSM_P = 64
D_FF = 4 * D_MODEL
IN_W = ATTN_W + 2 * KV_W + SSM_W
EPS = 1e-6
DT_MIN = 1e-3
DT_MAX = 1e-1
EIG_CLIP = -1e-4
NEG_INF = -1e30

SUBLANES = 8
LANES = 128
VMEM_LIMIT_BYTES = 56 * 1024 * 1024

BG = SUBLANES
ROWS = BG * CHUNK
QK_W = ATTN_W + KV_W
V_OFF = QK_W
U_OFF = QK_W + KV_W
N_KEYS = N_META + WINDOW + CHUNK
N_SLOTS = WINDOW // CHUNK + 1
HALF_G = SSM_G // 2
HALF_RI = HALF_G * SSM_P
HALF_STATE = 2 * HALF_RI
STATE_W = 2 * HALF_STATE
HALF_U = HALF_G * SSM_CH
U_SLABS = SSM_W // LANES
SCAN_W = 512
MLP_ROWS = 512
E_TILE = 256
FF_BLK = 1024

BF16 = jnp.bfloat16
F32 = jnp.float32


def _rms(x, g):
    ms = jnp.mean(x * x, axis=-1, keepdims=True)
    return x * lax.rsqrt(ms + EPS) * g


def _project(x, g1_ref, win_ref, e_ref, gqk_ref):
    xn = _rms(x, g1_ref[...]).astype(BF16)
    z = jnp.dot(xn, win_ref[...], preferred_element_type=F32)
    zqk = z[:, :QK_W]
    sq = (zqk * zqk).astype(BF16)
    ms = jnp.concatenate(
        [jnp.dot(sq[:, o:o + n], e_ref[:n, :n], preferred_element_type=F32)
         for o, n in ((0, E_TILE), (E_TILE, E_TILE), (2 * E_TILE, QK_W - 2 * E_TILE))], axis=1)
    qk = zqk * lax.rsqrt(ms + EPS) * gqk_ref[...]
    return qk, z


def _meta_kernel(meta_ref, g1_ref, win_ref, e_ref, gqk_ref, a_ref, bbd_ref,
                 k_ref, v_ref, st_ref, bu_scr):
    qk, z = _project(meta_ref[...], g1_ref, win_ref, e_ref, gqk_ref)
    k_ref[...] = qk[:, ATTN_W:]
    v_ref[...] = z[:, V_OFF:U_OFF]
    u = z[:, U_OFF:].astype(BF16)
    for h in range(2):
        bu_scr[:, h * HALF_STATE:(h + 1) * HALF_STATE] = jnp.dot(
            u[:, h * HALF_U:(h + 1) * HALF_U], bbd_ref[h], preferred_element_type=F32)
    for h in range(2):
        ar = a_ref[2 * h:2 * h + 1, :]
        ai = a_ref[2 * h + 1:2 * h + 2, :]
        cr = h * HALF_STATE
        ci = cr + HALF_RI
        xr = jnp.zeros((1, HALF_RI), F32)
        xi = jnp.zeros((1, HALF_RI), F32)
        for t in range(N_META):
            br = bu_scr[t:t + 1, cr:cr + HALF_RI]
            bi = bu_scr[t:t + 1, ci:ci + HALF_RI]
            xr, xi = ar * xr - ai * xi + br, ar * xi + ai * xr + bi
        st_ref[:, cr:cr + HALF_RI] = xr
        st_ref[:, ci:ci + HALF_RI] = xi


def _mixer_kernel(x_ref, st0_ref, kmeta_ref, vmeta_ref, kprev_ref, vprev_ref,
                  g1_ref, win_ref, e_ref, gqk_ref, sinks_ref, a_ref, bbd_ref, ccat_ref, dvec_ref,
                  wglu_ref, bglu_ref, ga_ref, gs_ref, wout_ref,
                  h1_ref, kout_ref, vout_ref, st_ref,
                  attn_scr, utb_scr, bu_scr, ytb_scr, ybt_scr, kwin, vwin,
                  *, n_chunks, chunk_offset, n_out):
    c = pl.program_id(1)
    gc = c + chunk_offset

    @pl.when(c == 0)
    def _():
        st_ref[...] = st0_ref[...]
        kwin[:, 0:N_META, :] = kmeta_ref[...].astype(BF16)
        vwin[:, 0:N_META, :] = vmeta_ref[...].astype(BF16)
        kwin[:, N_META:N_META + WINDOW, :] = kprev_ref[...].astype(BF16)
        vwin[:, N_META:N_META + WINDOW, :] = vprev_ref[...].astype(BF16)
        kwin[:, N_META + WINDOW:, :] = jnp.zeros((BG, CHUNK, KV_W), BF16)
        vwin[:, N_META + WINDOW:, :] = jnp.zeros((BG, CHUNK, KV_W), BF16)

    x = x_ref[...].reshape(ROWS, D_MODEL)
    qk, z = _project(x, g1_ref, win_ref, e_ref, gqk_ref)
    k =qk[:, ATTN_W:].reshape(BG, CHUNK, KV_W)
    v = z[:, V_OFF:U_OFF].reshape(BG, CHUNK, KV_W)
    u = z[:, U_OFF:]

    slot_row = pl.multiple_of(N_META + lax.rem(gc, N_SLOTS) * CHUNK, 16)
    kwin[:, pl.ds(slot_row, CHUNK), :] = k.astype(BF16)
    vwin[:, pl.ds(slot_row, CHUNK), :] = v.astype(BF16)
    for j in range(n_out):
        @pl.when(c == n_chunks - n_out + j)
        def _():
            kout_ref[:, j * CHUNK:(j + 1) * CHUNK, :] = k
            vout_ref[:, j * CHUNK:(j + 1) * CHUNK, :] = v

    key_limit = jnp.where(gc >= N_SLOTS - 1, N_KEYS, N_META + CHUNK * (gc + 1))
    key_ok = lax.broadcasted_iota(jnp.int32, (1, N_KEYS), 1) < key_limit
    lane = lax.broadcasted_iota(jnp.int32, (CHUNK, LANES), 1)
    head_lanes = (lane < HEAD_DIM, lane >= HEAD_DIM)

    for b in range(BG):
        r0 = b * CHUNK
        qb = qk[r0:r0 + CHUNK, 0:ATTN_W]
        parts = [jnp.where(head_lanes[kvh], qb[:, r * LANES:(r + 1) * LANES], 0.0)
                 for kvh in range(N_KV_HEADS) for r in range(REP)]
        qs = jnp.concatenate(parts, axis=0).astype(BF16)
        kb = kwin[b]
        vb = vwin[b]
        s = lax.dot_general(qs, kb, (((1,), (1,)), ((), ())), preferred_element_type=F32)
        s = jnp.where(key_ok, s, NEG_INF)
        p_parts = []
        inv_l = []
        for i in range(N_Q_HEADS):
            sb = s[i * CHUNK:(i + 1) * CHUNK]
            sink = sinks_ref[i]
            m = jnp.maximum(jnp.max(sb, axis=-1, keepdims=True), sink)
            p = jnp.exp(sb - m)
            l = jnp.sum(p, axis=-1, keepdims=True) + jnp.exp(sink - m)
            p_parts.append(p.astype(BF16))
            inv_l.append(1.0 / l)
        o = jnp.dot(jnp.concatenate(p_parts, axis=0), vb, preferred_element_type=F32)
        outs = []
        for r in range(REP):
            o0 = o[r * CHUNK:(r + 1) * CHUNK] * inv_l[r]
            o1 = o[(REP + r) * CHUNK:(REP + r + 1) * CHUNK] * inv_l[REP + r]
            outs.append(jnp.where(head_lanes[0], o0, o1))
        attn_scr[r0:r0 + CHUNK, :] = jnp.concatenate(outs, axis=1)

    for b in range(BG):
        for j in range(U_SLABS):
            utb_scr[j, pl.ds(b, CHUNK, stride=BG), :] = (
                u[b * CHUNK:(b + 1) * CHUNK, j * LANES:(j + 1) * LANES])
    for h in range(2):
        uh = jnp.concatenate([utb_scr[2 * h], utb_scr[2 * h + 1]], axis=1).astype(BF16)
        bu_scr[:, h * HALF_STATE:(h + 1) * HALF_STATE] = jnp.dot(
            uh, bbd_ref[h], preferred_element_type=F32)

    for h in range(2):
        for w in range(HALF_RI // SCAN_W):
            cr = h * HALF_STATE + w * SCAN_W
            ci = cr + HALF_RI
            ar = jnp.broadcast_to(a_ref[2 * h:2 * h + 1, w * SCAN_W:(w + 1) * SCAN_W], (BG, SCAN_W))
            ai = jnp.broadcast_to(a_ref[2 * h + 1:2 * h + 2, w * SCAN_W:(w + 1) * SCAN_W], (BG, SCAN_W))

            def scan_body(t, carry, cr=cr, ci=ci, ar=ar, ai=ai):
                xr, xi = carry
                r = pl.multiple_of(t * BG, BG)
                br = bu_scr[pl.ds(r, BG), cr:cr + SCAN_W]
                bi = bu_scr[pl.ds(r, BG), ci:ci + SCAN_W]
                nr = ar * xr - ai * xi + br
                ni = ar * xi + ai * xr + bi
                bu_scr[pl.ds(r, BG), cr:cr + SCAN_W] = nr
                bu_scr[pl.ds(r, BG), ci:ci + SCAN_W] = ni
                return nr, ni

            xr, xi = lax.fori_loop(
                0, CHUNK, scan_body,
                (st_ref[:, cr:cr + SCAN_W], st_ref[:, ci:ci + SCAN_W]), unroll=True)
            st_ref[:, cr:cr + SCAN_W] = xr
            st_ref[:, ci:ci + SCAN_W] = xi

    for h in range(2):
        yh = jnp.dot(bu_scr[:, h * HALF_STATE:(h + 1) * HALF_STATE].astype(BF16), ccat_ref[h],
                     preferred_element_type=F32)
        for jj in range(2):
            ytb_scr[2 * h + jj] = yh[:, jj * LANES:(jj + 1) * LANES]
    for b in range(BG):
        for j in range(U_SLABS):
            ybt_scr[b * CHUNK:(b + 1) * CHUNK, j * LANES:(j + 1) * LANES] = (
                ytb_scr[j, pl.ds(b, CHUNK, stride=BG), :])

    ys = ybt_scr[...] + dvec_ref[...] * u
    zs = jax.nn.gelu(ys)
    gate = jnp.dot(zs.astype(BF16), wglu_ref[...], preferred_element_type=F32) + bglu_ref[...]
    s_out = zs * jax.nn.sigmoid(gate)
    merged = jnp.concatenate([_rms(attn_scr[...], ga_ref[...]), _rms(s_out, gs_ref[...])], axis=1)
    h1 = x + jnp.dot(merged.astype(BF16), wout_ref[...], preferred_element_type=F32)
    h1_ref[...] = h1.reshape(BG, CHUNK, D_MODEL)


def _mlp_kernel(h_ref, g2_ref, wup_ref, wdown_ref, o_ref):
    h = h_ref[...]
    xn = _rms(h, g2_ref[...]).astype(BF16)
    acc = h
    for j in range(D_FF // FF_BLK):
        f = jnp.dot(xn, wup_ref[:, j * FF_BLK:(j + 1) * FF_BLK], preferred_element_type=F32)
        f = jnp.maximum(f, 0.0)
        acc = acc + jnp.dot((f * f).astype(BF16), wdown_ref[j * FF_BLK:(j + 1) * FF_BLK, :],
                            preferred_element_type=F32)
    o_ref[...] = acc


def _const_spec(shape):
    n = len(shape)
    return pl.BlockSpec(shape, lambda *_: (0,) * n, pipeline_mode=pl.Buffered(1))


def _meta_call(meta, w):
    return pl.pallas_call(
        _meta_kernel,
        out_shape=(jax.ShapeDtypeStruct((N_META, KV_W), F32),
                   jax.ShapeDtypeStruct((N_META, KV_W), F32),
                   jax.ShapeDtypeStruct((1, STATE_W), F32)),
        scratch_shapes=[pltpu.VMEM((N_META, STATE_W), F32)],
        compiler_params=pltpu.CompilerParams(vmem_limit_bytes=VMEM_LIMIT_BYTES),
        name="meta",
    )(meta, w["g1"], w["w_in"], w["e"], w["gqk"], w["a"], w["bbd"])


def _mixer_call(x, st0, kmeta, vmeta, kprev, vprev, w, *, chunk_offset, n_out):
    nb, s_len, _ = x.shape
    n_chunks = s_len // CHUNK
    grid = (nb // BG, n_chunks)
    per_bg = lambda bg, c: (bg, 0, 0)
    in_specs = [
        pl.BlockSpec((BG, CHUNK, D_MODEL), lambda bg, c: (bg, c, 0)),
        pl.BlockSpec((BG, STATE_W), lambda bg, c: (bg, 0)),
        pl.BlockSpec((BG, N_META, KV_W), per_bg),
        pl.BlockSpec((BG, N_META, KV_W), per_bg),
        pl.BlockSpec((BG, WINDOW, KV_W), per_bg),
        pl.BlockSpec((BG, WINDOW, KV_W), per_bg),
        _const_spec((1, D_MODEL)),
        _const_spec((D_MODEL, IN_W)),
        _const_spec((E_TILE, E_TILE)),
        _const_spec((1, QK_W)),
        pl.BlockSpec(memory_space=pltpu.SMEM),
        _const_spec((4, HALF_RI)),
        _const_spec((2, HALF_U, HALF_STATE)),
        _const_spec((2, HALF_STATE, HALF_U)),
        _const_spec((1, SSM_W)),
        _const_spec((SSM_W, SSM_W)),
        _const_spec((1, SSM_W)),
        _const_spec((1, ATTN_W)),
        _const_spec((1, SSM_W)),
        _const_spec((D_MODEL, D_MODEL)),
    ]
    out_specs = (
        pl.BlockSpec((BG, CHUNK, D_MODEL), lambda bg, c: (bg, c, 0)),
        pl.BlockSpec((BG, n_out * CHUNK, KV_W), per_bg),
        pl.BlockSpec((BG, n_out * CHUNK, KV_W), per_bg),
        pl.BlockSpec((BG, STATE_W), lambda bg, c: (bg, 0)),
    )
    out_shape = (
        jax.ShapeDtypeStruct((nb, s_len, D_MODEL), F32),
        jax.ShapeDtypeStruct((nb, n_out * CHUNK, KV_W), F32),
        jax.ShapeDtypeStruct((nb, n_out * CHUNK, KV_W), F32),
        jax.ShapeDtypeStruct((nb, STATE_W), F32),
    )
    scratch = [
        pltpu.VMEM((ROWS, ATTN_W), F32),
        pltpu.VMEM((U_SLABS, ROWS, LANES), F32),
        pltpu.VMEM((ROWS, STATE_W), F32),
        pltpu.VMEM((U_SLABS, ROWS, LANES), F32),
        pltpu.VMEM((ROWS, SSM_W), F32),
        pltpu.VMEM((BG, N_KEYS, KV_W), BF16),
        pltpu.VMEM((BG, N_KEYS, KV_W), BF16),
    ]
    kern = functools.partial(_mixer_kernel, n_chunks=n_chunks, chunk_offset=chunk_offset, n_out=n_out)
    return pl.pallas_call(
        kern, grid=grid, in_specs=in_specs, out_specs=out_specs, out_shape=out_shape,
        scratch_shapes=scratch,
        compiler_params=pltpu.CompilerParams(
            dimension_semantics=("arbitrary", "arbitrary"), vmem_limit_bytes=VMEM_LIMIT_BYTES),
        name="mixer",
    )(x, st0, kmeta, vmeta, kprev, vprev,
      w["g1"], w["w_in"], w["e"], w["gqk"], w["sinks"], w["a"], w["bbd"], w["ccat"], w["dvec"],
      w["w_glu"], w["b_glu"], w["ga"], w["gs"], w["w_out"])


def _mlp_call(h, w):
    n = h.shape[0]
    return pl.pallas_call(
        _mlp_kernel,
        grid=(n // MLP_ROWS,),
        in_specs=[pl.BlockSpec((MLP_ROWS, D_MODEL), lambda i: (i, 0)),
                  _const_spec((1, D_MODEL)),
                  _const_spec((D_MODEL, D_FF)),
                  _const_spec((D_FF, D_MODEL))],
        out_specs=pl.BlockSpec((MLP_ROWS, D_MODEL), lambda i: (i, 0)),
        out_shape=jax.ShapeDtypeStruct((n, D_MODEL), F32),
        compiler_params=pltpu.CompilerParams(
            dimension_semantics=("arbitrary",), vmem_limit_bytes=VMEM_LIMIT_BYTES),
        name="mlp",
    )(h, w["g2"], w["w_up"], w["w_down"])


def _prep_weights(norm1_g, w_in, q_norm_g, k_norm_g, sinks, ssm_A_re, ssm_A_im, ssm_log_dt,
                  ssm_B_re, ssm_B_im, ssm_C_re, ssm_C_im, ssm_D, w_glu, b_glu, attn_out_g,
                  ssm_out_g, w_out, norm2_g, w_up, w_down):
    def attn_perm(m):
        lead = m.shape[:-1]
        return m.reshape(lead + (N_KV_HEADS, REP, HEAD_DIM)).swapaxes(-3, -2).reshape(lead + (ATTN_W,))

    w_in_p = jnp.concatenate([attn_perm(w_in[:, :ATTN_W]), w_in[:, ATTN_W:]], axis=1).astype(BF16)
    e = jnp.kron(jnp.eye(E_TILE // HEAD_DIM, dtype=F32),
                 jnp.full((HEAD_DIM, HEAD_DIM), 1.0 / HEAD_DIM, F32)).astype(BF16)
    gqk = jnp.concatenate([jnp.tile(q_norm_g, N_Q_HEADS) * (HEAD_DIM ** -0.5),
                           jnp.tile(k_norm_g, N_KV_HEADS)])[None, :]

    lam_re = jnp.minimum(ssm_A_re, EIG_CLIP)
    lam_im = ssm_A_im
    dt = jnp.exp(ssm_log_dt)[:, None]
    mag = jnp.exp(lam_re * dt)
    ar = mag * jnp.cos(lam_im * dt)
    ai = mag * jnp.sin(lam_im * dt)
    den = lam_re * lam_re + lam_im * lam_im
    cr = ((ar - 1.0) * lam_re + ai * lam_im) / den
    ci = (ai * lam_re - (ar - 1.0) * lam_im) / den
    bb_re = cr[..., None] * ssm_B_re - ci[..., None] * ssm_B_im
    bb_im = cr[..., None] * ssm_B_im + ci[..., None] * ssm_B_re
    a = jnp.stack([ar.reshape(2, HALF_RI), ai.reshape(2, HALF_RI)], axis=1).reshape(4, HALF_RI)
    eye = jnp.eye(HALF_G, dtype=F32)
    bb = jnp.stack([bb_re, bb_im], 0).reshape(2, 2, HALF_G, SSM_P, SSM_CH)
    bbd = jnp.einsum('ab,rhapc->hacrbp', eye, bb).reshape(2, HALF_U, HALF_STATE).astype(BF16)
    cc = jnp.stack([ssm_C_re, -ssm_C_im], 0).reshape(2, 2, HALF_G, SSM_CH, SSM_P)
    ccat = jnp.einsum('ab,rhacp->hrapbc', eye, cc).reshape(2, HALF_STATE, HALF_U).astype(BF16)

    w_out_p = jnp.concatenate(
        [attn_perm(w_out[:ATTN_W].T).T, w_out[ATTN_W:]], axis=0).astype(BF16)
    return {
        "g1": norm1_g[None, :], "w_in": w_in_p, "e": e, "gqk": gqk, "sinks": sinks,
        "a": a, "bbd": bbd, "ccat": ccat, "dvec": ssm_D.reshape(1, SSM_W),
        "w_glu": w_glu.astype(BF16), "b_glu": b_glu[None, :],
        "ga": attn_perm(attn_out_g)[None, :], "gs": ssm_out_g[None, :], "w_out": w_out_p,
        "g2": norm2_g[None, :], "w_up": w_up.astype(BF16), "w_down": w_down.astype(BF16),
    }


def _state_to_lanes(re, im):
    nb = re.shape[0]
    return jnp.stack([re.reshape(nb, 2, HALF_RI), im.reshape(nb, 2, HALF_RI)], axis=2).reshape(nb, STATE_W)


def _lanes_to_state(st):
    nb = st.shape[0]
    s = st.reshape(nb, 2, 2, HALF_RI)
    return s[:, :, 0].reshape(nb, SSM_G, SSM_P), s[:, :, 1].reshape(nb, SSM_G, SSM_P)


def kernel(x_prompt, x_sample, cache_swa_k, cache_swa_v, state_ssm_re, state_ssm_im, meta_tokens, norm1_g, w_in, q_norm_g, k_norm_g, sinks, ssm_A_re, ssm_A_im, ssm_log_dt, ssm_B_re, ssm_B_im, ssm_C_re, ssm_C_im, ssm_D, w_glu, b_glu, attn_out_g, ssm_out_g, w_out, norm2_g, w_up, w_down):
    assert norm1_g.shape[0] == 1, "single-layer trunk"
    nb, seq, _ = x_prompt.shape
    nd, dseq, _ = x_sample.shape
    assert dseq == CHUNK and seq % CHUNK == 0 and seq >= WINDOW and nb % BG == 0 and nd % BG == 0
    w = _prep_weights(norm1_g[0], w_in[0], q_norm_g[0], k_norm_g[0], sinks[0], ssm_A_re[0],
                      ssm_A_im[0], ssm_log_dt[0], ssm_B_re[0], ssm_B_im[0], ssm_C_re[0],
                      ssm_C_im[0], ssm_D[0], w_glu[0], b_glu[0], attn_out_g[0], ssm_out_g[0],
                      w_out[0], norm2_g[0], w_up[0], w_down[0])

    k_meta, v_meta, st_meta = _meta_call(meta_tokens, w)
    kmeta_b = jnp.broadcast_to(k_meta[None], (nb, N_META, KV_W))
    vmeta_b = jnp.broadcast_to(v_meta[None], (nb, N_META, KV_W))
    zeros_win = jnp.zeros((nb, WINDOW, KV_W), F32)
    h1p, kp, vp, stp = _mixer_call(
        x_prompt, jnp.broadcast_to(st_meta, (nb, STATE_W)), kmeta_b, vmeta_b, zeros_win, zeros_win,
        w, chunk_offset=0, n_out=WINDOW // CHUNK)
    y_prompt = _mlp_call(h1p.reshape(nb * seq, D_MODEL), w).reshape(nb, seq, D_MODEL)

    ck = cache_swa_k[0].reshape(nd, N_META + WINDOW, KV_W)
    cv = cache_swa_v[0].reshape(nd, N_META + WINDOW, KV_W)
    h1s, ks, vs, sts = _mixer_call(
        x_sample, _state_to_lanes(state_ssm_re[0], state_ssm_im[0]),
        ck[:, :N_META], cv[:, :N_META], ck[:, N_META:], cv[:, N_META:],
        w, chunk_offset=WINDOW // CHUNK, n_out=1)
    y_sample = _mlp_call(h1s.reshape(nd * dseq, D_MODEL), w).reshape(nd, dseq, D_MODEL)

    kv5 = lambda t: t.reshape(t.shape[0], t.shape[1], N_KV_HEADS, HEAD_DIM)[None]
    new_k_prompt = kv5(jnp.concatenate([kmeta_b, kp], axis=1))
    new_v_prompt = kv5(jnp.concatenate([vmeta_b, vp], axis=1))
    srp, sip = _lanes_to_state(stp)
    srs, sis = _lanes_to_state(sts)
    return (y_prompt, y_sample, new_k_prompt, new_v_prompt, srp[None], sip[None],
            kv5(ks), kv5(vs), srs[None], sis[None])
```

```python
import functools

import jax
import jax.numpy as jnp
from jax import lax
from jax.experimental import pallas as pl
from jax.experimental.pallas import tpu as pltpu

D_MODEL = 1024
CHUNK = 64
N_META = 16
HEAD_DIM = 64
ATTN_W = 512
N_KV_HEADS = 2
REP = 4
N_Q_HEADS = N_KV_HEADS * REP
KV_W = N_KV_HEADS * HEAD_DIM
WINDOW = 128
SSM_W = 512
SSM_CH = 16
SSM_G = 32
SSM_P = 64
D_FF = 4 * D_MODEL
IN_W = ATTN_W + 2 * KV_W + SSM_W
EPS = 1e-6
DT_MIN = 1e-3
DT_MAX = 1e-1
EIG_CLIP = -1e-4
NEG_INF = -1e30

SUBLANES = 8
LANES = 128
VMEM_LIMIT_BYTES = 56 * 1024 * 1024

BG = SUBLANES
ROWS = BG * CHUNK
QK_W = ATTN_W + KV_W
V_OFF = QK_W
U_OFF = QK_W + KV_W
N_KEYS = N_META + WINDOW + CHUNK
KEY_ROWS = 256
LOG2E = 1.4426950408889634
N_SLOTS = WINDOW // CHUNK + 1
HALF_G = SSM_G // 2
HALF_RI = HALF_G * SSM_P
HALF_STATE = 2 * HALF_RI
STATE_W = 2 * HALF_STATE
HALF_U = HALF_G * SSM_CH
U_SLABS = SSM_W // LANES
SCAN_W = 512
MLP_ROWS = 512
E_TILE = 256
MXU_N = 256
FF_BLK = 1024

BF16 = jnp.bfloat16
F32 = jnp.float32

_STEP_ORDER = (
    "unpack norm OP:0 OP:1 BU:0 S:0 BU:1 SCAN:0 S:1 OP:2 PV:0 S:2 OP:3 SCAN:1 PV:1 S:3 W:0 PV:2 S:4 W:1 "
    "PV:3 S:5 W:2 PV:4 S:6 QKN PV:5 S:7 W:3 PV:6 C:0 PV:7 C:1 W:4 GLU_IN GLU_OUT AN")


def _rms(x, g):
    ms = jnp.mean(x * x, axis=-1, keepdims=True)
    return x * lax.rsqrt(ms + EPS) * g


def _project(x, g1_ref, win_ref, e_ref, gqk_ref):
    xn = _rms(x, g1_ref[...]).astype(BF16)
    z = jnp.dot(xn, win_ref[...], preferred_element_type=F32)
    zqk = z[:, :QK_W]
    sq = (zqk * zqk).astype(BF16)
    ms = jnp.concatenate(
        [jnp.dot(sq[:, o:o + n], e_ref[:n, :n], preferred_element_type=F32)
         for o, n in ((0, E_TILE), (E_TILE, E_TILE), (2 * E_TILE, QK_W - 2 * E_TILE))], axis=1)
    qk = zqk * lax.rsqrt(ms + EPS) * gqk_ref[...]
    return qk, z


def _meta_kernel(meta_ref, g1_ref, win_ref, e_ref, gqk_ref, a_ref, bbd_ref,
                 k_ref, v_ref, st_ref, bu_scr):
    qk, z = _project(meta_ref[...], g1_ref, win_ref, e_ref, gqk_ref)
    k_ref[...] = qk[:, ATTN_W:]
    v_ref[...] = z[:, V_OFF:U_OFF]
    u = z[:, U_OFF:].astype(BF16)
    for h in range(2):
        bu_scr[:, h * HALF_STATE:(h + 1) * HALF_STATE] = jnp.dot(
            u[:, h * HALF_U:(h + 1) * HALF_U], bbd_ref[h], preferred_element_type=F32)
    for h in range(2):
        ar = a_ref[2 * h:2 * h + 1, :]
        ai = a_ref[2 * h + 1:2 * h + 2, :]
        cr = h * HALF_STATE
        ci = cr + HALF_RI
        xr = jnp.zeros((1, HALF_RI), F32)
        xi = jnp.zeros((1, HALF_RI), F32)
        for t in range(N_META):
            br = bu_scr[t:t + 1, cr:cr + HALF_RI]
            bi = bu_scr[t:t + 1, ci:ci + HALF_RI]
            xr, xi = ar * xr - ai * xi + br, ar * xi + ai * xr + bi
        st_ref[:, cr:cr + HALF_RI] = xr
        st_ref[:, ci:ci + HALF_RI] = xi


def _mixer_kernel(x_ref, st0_ref, kmeta_ref, vmeta_ref, kprev_ref, vprev_ref,
                  g1_ref, win_ref, e_ref, gqk_ref, fill_ref, row0_ref, a_ref, bbd_ref, ccat_ref, dvec_ref,
                  wglu_ref, bglu_ref, ga_ref, gs_ref, wout_ref,
                  d_ref, kout_ref, vout_ref, st_ref,
                  z_scr, xn_scr, qs_scr, du_scr, attn_scr, utb_scr, bu_scr, ytb_scr, ybt_scr, kwin, vwin,
                  merged_scr, *, n_chunks, n_bg, chunk_offset, n_out):
    s = pl.program_id(0)
    n_steps = n_bg * n_chunks
    c = _mix_chunk(s, n_steps) % n_chunks
    live = jnp.logical_and(s >= 1, s <= n_steps)
    gc = c + chunk_offset
    row0 = row0_ref[0]

    @pl.when(s == 0)
    def _():
        z_scr[...] = jnp.zeros((ROWS, IN_W), F32)
        merged_scr[...] = jnp.zeros((ROWS, D_MODEL), BF16)

    @pl.when(jnp.logical_and(c == 0, s <= n_steps))
    def _():
        st_ref[...] = st0_ref[...]
        kwin[:, 0:N_META, :] = kmeta_ref[...].astype(BF16)
        vwin[:, 0:N_META, :] = vmeta_ref[...].astype(BF16)
        kwin[:, N_META:N_META + WINDOW, :] = kprev_ref[...].astype(BF16)
        vwin[:, N_META:N_META + WINDOW, :] = vprev_ref[...].astype(BF16)
        kwin[:, N_META + WINDOW:, :] = jnp.zeros((BG, KEY_ROWS - N_META - WINDOW, KV_W), BF16)
        vwin[:, N_META + WINDOW:, :] = jnp.zeros((BG, KEY_ROWS - N_META - WINDOW, KV_W), BF16)

    lane = lax.broadcasted_iota(jnp.int32, (CHUNK, LANES), 1)
    head_lanes = (lane < HEAD_DIM, lane >= HEAD_DIM)
    key_limit = jnp.where(gc >= N_SLOTS - 1, N_KEYS, N_META + CHUNK * (gc + 1))
    key_ok = lax.broadcasted_iota(jnp.int32, (1, KEY_ROWS), 1) < key_limit

    def unpack():
        k = z_scr[:, ATTN_W:QK_W].reshape(BG, CHUNK, KV_W)
        v = z_scr[:, V_OFF:U_OFF].reshape(BG, CHUNK, KV_W)
        u = z_scr[:, U_OFF:]
        slot_row = pl.multiple_of(N_META + lax.rem(gc, N_SLOTS) * CHUNK, 16)
        kwin[:, pl.ds(slot_row, CHUNK), :] = k.astype(BF16)
        vwin[:, pl.ds(slot_row, CHUNK), :] = v.astype(BF16)
        out_row = pl.multiple_of(lax.rem(c, n_out) * CHUNK, CHUNK)
        kout_ref[:, pl.ds(out_row, CHUNK), :] = k
        vout_ref[:, pl.ds(out_row, CHUNK), :] = v
        for b in range(BG):
            qb = z_scr[b * CHUNK:(b + 1) * CHUNK, 0:ATTN_W]
            parts = [jnp.where(head_lanes[kvh], qb[:, r * LANES:(r + 1) * LANES], 0.0)
                     for kvh in range(N_KV_HEADS) for r in range(REP)]
            qs_scr[b] = jnp.concatenate(parts, axis=0).astype(BF16)
        for b in range(BG):
            for j in range(U_SLABS):
                utb_scr[j, pl.ds(b, CHUNK, stride=BG), :] = (
                    u[b * CHUNK:(b + 1) * CHUNK, j * LANES:(j + 1) * LANES])
        du_scr[...] = dvec_ref[...] * u

    def out_proj(j):
        cols = slice(j * MXU_N, (j + 1) * MXU_N)
        d_ref[:, :, cols] = jnp.dot(merged_scr[...], wout_ref[:, cols],
                                    preferred_element_type=F32).reshape(BG, CHUNK, MXU_N)

    def proj_norm():
        xn_scr[...] = _rms(x_ref[...].reshape(ROWS, D_MODEL), g1_ref[...]).astype(BF16)

    def proj(j):
        cols = slice(j * MXU_N, (j + 1) * MXU_N)
        z_scr[:, cols] = jnp.dot(xn_scr[...], win_ref[:, cols], preferred_element_type=F32)

    def proj_qknorm():
        for o, n in ((0, E_TILE), (E_TILE, E_TILE), (2 * E_TILE, QK_W - 2 * E_TILE)):
            zq = z_scr[:, o:o + n]
            ms = jnp.dot((zq * zq).astype(BF16), e_ref[:n, :n], preferred_element_type=F32)
            z_scr[:, o:o + n] = zq * lax.rsqrt(ms + EPS) * gqk_ref[:, o:o + n]

    probs = {}

    def attn_scores(b):
        sc = lax.dot_general(qs_scr[b], kwin[b], (((1,), (1,)), ((), ())), preferred_element_type=F32)
        p_parts = []
        inv_l = []
        for i in range(N_Q_HEADS):
            sb = jnp.where(key_ok, sc[i * CHUNK:(i + 1) * CHUNK], fill_ref[i:i + 1, :])
            m = jnp.max(sb, axis=-1, keepdims=True)
            p = jnp.exp2(sb - m)
            l = jnp.sum(p, axis=-1, keepdims=True)
            p_parts.append(p.astype(BF16))
            inv_l.append(1.0 / l)
        probs[b] = (jnp.concatenate(p_parts, axis=0), inv_l)

    def attn_values(b):
        p_all, inv_l = probs.pop(b)
        o = jnp.dot(p_all, vwin[b], preferred_element_type=F32)
        outs = []
        for r in range(REP):
            o0 = o[r * CHUNK:(r + 1) * CHUNK] * inv_l[r]
            o1 = o[(REP + r) * CHUNK:(REP + r + 1) * CHUNK] * inv_l[REP + r]
            outs.append(jnp.where(head_lanes[0], o0, o1))
        attn_scr[b * CHUNK:(b + 1) * CHUNK, :] = jnp.concatenate(outs, axis=1)

    def s5_in(h):
        uh = jnp.concatenate([utb_scr[2 * h], utb_scr[2 * h + 1]], axis=1).astype(BF16)
        bu_scr[:, h * HALF_STATE:(h + 1) * HALF_STATE] = jnp.dot(
            uh, bbd_ref[h], preferred_element_type=F32)

    def s5_scan(h):
        for w in range(HALF_RI // SCAN_W):
            cr = h * HALF_STATE + w * SCAN_W
            ci = cr + HALF_RI
            ar = jnp.broadcast_to(a_ref[2 * h:2 * h + 1, w * SCAN_W:(w + 1) * SCAN_W], (BG, SCAN_W))
            ai = jnp.broadcast_to(a_ref[2 * h + 1:2 * h + 2, w * SCAN_W:(w + 1) * SCAN_W], (BG, SCAN_W))

            def scan_body(t, carry, cr=cr, ci=ci, ar=ar, ai=ai):
                xr, xi = carry
                r = pl.multiple_of(t * BG, BG)
                rl = pl.multiple_of(row0 + t * BG, BG)
                br = bu_scr[pl.ds(rl, BG), cr:cr + SCAN_W]
                bi = bu_scr[pl.ds(rl, BG), ci:ci + SCAN_W]
                nr = ar * xr - ai * xi + br
                ni = ar * xi + ai * xr + bi
                bu_scr[pl.ds(r, BG), cr:cr + SCAN_W] = nr
                bu_scr[pl.ds(r, BG), ci:ci + SCAN_W] = ni
                return nr, ni

            xr0 = st_ref[:, cr:cr + SCAN_W]
            xi0 = st_ref[:, ci:ci + SCAN_W]
            xr, xi = lax.fori_loop(0, CHUNK, scan_body, (xr0, xi0), unroll=True)
            st_ref[:, cr:cr + SCAN_W] = jnp.where(live, xr, xr0)
            st_ref[:, ci:ci + SCAN_W] = jnp.where(live, xi, xi0)

    def s5_out(h):
        xs = bu_scr[pl.ds(pl.multiple_of(row0, ROWS), ROWS), h * HALF_STATE:(h + 1) * HALF_STATE]
        yh = jnp.dot(xs.astype(BF16), ccat_ref[h],
                     preferred_element_type=F32)
        for jj in range(2):
            ytb_scr[2 * h + jj] = yh[:, jj * LANES:(jj + 1) * LANES]

    def glu_in():
        for b in range(BG):
            for j in range(U_SLABS):
                ybt_scr[b * CHUNK:(b + 1) * CHUNK, j * LANES:(j + 1) * LANES] = (
                    ytb_scr[j, pl.ds(b, CHUNK, stride=BG), :])
        ybt_scr[...] = jax.nn.gelu(ybt_scr[...] + du_scr[...])

    def glu_out():
        zs = ybt_scr[...]
        gate = jnp.dot(zs.astype(BF16), wglu_ref[...], preferred_element_type=F32) + bglu_ref[...]
        s_out = zs * jax.nn.sigmoid(gate)
        merged_scr[:, ATTN_W:] = _rms(s_out, gs_ref[...]).astype(BF16)

    def attn_norm():
        merged_scr[:, :ATTN_W] = _rms(attn_scr[...], ga_ref[...]).astype(BF16)

    pieces = {"unpack": unpack, "norm": proj_norm, "W": proj, "QKN": proj_qknorm, "OP": out_proj,
              "S": attn_scores, "PV": attn_values, "BU": s5_in, "SCAN": s5_scan, "C": s5_out,
              "GLU_IN": glu_in, "GLU_OUT": glu_out, "AN": attn_norm}
    for item in _STEP_ORDER.split():
        name, _, arg = item.partition(":")
        pieces[name](*([int(arg)] if arg else []))


def _mlp_kernel(x_ref, d_ref, g2_ref, wup_ref, wdown_ref, o_ref):
    h = x_ref[...] + d_ref[...]
    xn = _rms(h, g2_ref[...]).astype(BF16)
    acc = h
    for j in range(D_FF // FF_BLK):
        f = jnp.dot(xn, wup_ref[:, j * FF_BLK:(j + 1) * FF_BLK], preferred_element_type=F32)
        f = jnp.maximum(f, 0.0)
        acc = acc + jnp.dot((f * f).astype(BF16), wdown_ref[j * FF_BLK:(j + 1) * FF_BLK, :],
                            preferred_element_type=F32)
    o_ref[...] = acc


def _mix_chunk(s, n_steps):
    return jnp.clip(s - 1, 0, n_steps - 1)


def _const_spec(shape):
    n = len(shape)
    return pl.BlockSpec(shape, lambda *_: (0,) * n, pipeline_mode=pl.Buffered(1))


def _meta_call(meta, w):
    return pl.pallas_call(
        _meta_kernel,
        out_shape=(jax.ShapeDtypeStruct((N_META, KV_W), F32),
                   jax.ShapeDtypeStruct((N_META, KV_W), F32),
                   jax.ShapeDtypeStruct((1, STATE_W), F32)),
        scratch_shapes=[pltpu.VMEM((N_META, STATE_W), F32)],
        compiler_params=pltpu.CompilerParams(vmem_limit_bytes=VMEM_LIMIT_BYTES),
        name="meta",
    )(meta, w["g1"], w["w_in"], w["e"], w["gqk"], w["a"], w["bbd"])


def _mixer_call(x, st0, kmeta, vmeta, kprev, vprev, w, *, chunk_offset, n_out):
    nb, s_len, _ = x.shape
    n_chunks = s_len // CHUNK
    n_bg = nb // BG
    n_steps = n_bg * n_chunks
    assert n_chunks % n_out == 0
    grid = (n_steps + 2,)

    def chunk_block(gidx):
        return (gidx // n_chunks, gidx % n_chunks, 0)

    x_index = lambda s: chunk_block(jnp.minimum(s, n_steps - 1))
    d_index = lambda s: chunk_block(jnp.clip(s - 2, 0, n_steps - 1))
    per_bg = lambda s: (_mix_chunk(s, n_steps) // n_chunks, 0, 0)
    per_bg2 = lambda s: (_mix_chunk(s, n_steps) // n_chunks, 0)
    in_specs = [
        pl.BlockSpec((BG, CHUNK, D_MODEL), x_index),
        pl.BlockSpec((BG, STATE_W), per_bg2),
        pl.BlockSpec((BG, N_META, KV_W), per_bg),
        pl.BlockSpec((BG, N_META, KV_W), per_bg),
        pl.BlockSpec((BG, WINDOW, KV_W), per_bg),
        pl.BlockSpec((BG, WINDOW, KV_W), per_bg),
        _const_spec((1, D_MODEL)),
        _const_spec((D_MODEL, IN_W)),
        _const_spec((E_TILE, E_TILE)),
        _const_spec((1, QK_W)),
        _const_spec((N_Q_HEADS, KEY_ROWS)),
        pl.BlockSpec(memory_space=pltpu.SMEM),
        _const_spec((4, HALF_RI)),
        _const_spec((2, HALF_U, HALF_STATE)),
        _const_spec((2, HALF_STATE, HALF_U)),
        _const_spec((1, SSM_W)),
        _const_spec((SSM_W, SSM_W)),
        _const_spec((1, SSM_W)),
        _const_spec((1, ATTN_W)),
        _const_spec((1, SSM_W)),
        _const_spec((D_MODEL, D_MODEL)),
    ]
    out_specs = (
        pl.BlockSpec((BG, CHUNK, D_MODEL), d_index),
        pl.BlockSpec((BG, n_out * CHUNK, KV_W), per_bg),
        pl.BlockSpec((BG, n_out * CHUNK, KV_W), per_bg),
        pl.BlockSpec((BG, STATE_W), per_bg2),
    )
    out_shape = (
        jax.ShapeDtypeStruct((nb, s_len, D_MODEL), F32),
        jax.ShapeDtypeStruct((nb, n_out * CHUNK, KV_W), F32),
        jax.ShapeDtypeStruct((nb, n_out * CHUNK, KV_W), F32),
        jax.ShapeDtypeStruct((nb, STATE_W), F32),
    )
    scratch = [
        pltpu.VMEM((ROWS, IN_W), F32),
        pltpu.VMEM((ROWS, D_MODEL), BF16),
        pltpu.VMEM((BG, N_Q_HEADS * CHUNK, LANES), BF16),
        pltpu.VMEM((ROWS, SSM_W), F32),
        pltpu.VMEM((ROWS, ATTN_W), F32),
        pltpu.VMEM((U_SLABS, ROWS, LANES), F32),
        pltpu.VMEM((ROWS, STATE_W), F32),
        pltpu.VMEM((U_SLABS, ROWS, LANES), F32),
        pltpu.VMEM((ROWS, SSM_W), F32),
        pltpu.VMEM((BG, KEY_ROWS, KV_W), BF16),
        pltpu.VMEM((BG, KEY_ROWS, KV_W), BF16),
        pltpu.VMEM((ROWS, D_MODEL), BF16),
    ]
    kern = functools.partial(_mixer_kernel, n_chunks=n_chunks, n_bg=n_bg,
                             chunk_offset=chunk_offset, n_out=n_out)
    return pl.pallas_call(
        kern, grid=grid, in_specs=in_specs, out_specs=out_specs, out_shape=out_shape,
        scratch_shapes=scratch,
        compiler_params=pltpu.CompilerParams(
            dimension_semantics=("arbitrary",), vmem_limit_bytes=VMEM_LIMIT_BYTES),
        name="mixer",
    )(x, st0, kmeta, vmeta, kprev, vprev,
      w["g1"], w["w_in"], w["e"], w["gqk"], w["fill"], jnp.zeros((1,), jnp.int32),
      w["a"], w["bbd"], w["ccat"], w["dvec"],
      w["w_glu"], w["b_glu"], w["ga"], w["gs"], w["w_out"])


def _mlp_call(x, d, w):
    n = x.shape[0]
    return pl.pallas_call(
        _mlp_kernel,
        grid=(n // MLP_ROWS,),
        in_specs=[pl.BlockSpec((MLP_ROWS, D_MODEL), lambda i: (i, 0)),
                  pl.BlockSpec((MLP_ROWS, D_MODEL), lambda i: (i, 0)),
                  _const_spec((1, D_MODEL)),
                  _const_spec((D_MODEL, D_FF)),
                  _const_spec((D_FF, D_MODEL))],
        out_specs=pl.BlockSpec((MLP_ROWS, D_MODEL), lambda i: (i, 0)),
        out_shape=jax.ShapeDtypeStruct((n, D_MODEL), F32),
        compiler_params=pltpu.CompilerParams(
            dimension_semantics=("arbitrary",), vmem_limit_bytes=VMEM_LIMIT_BYTES),
        name="mlp",
    )(x, d, w["g2"], w["w_up"], w["w_down"])


def _prep_weights(norm1_g, w_in, q_norm_g, k_norm_g, sinks, ssm_A_re, ssm_A_im, ssm_log_dt,
                  ssm_B_re, ssm_B_im, ssm_C_re, ssm_C_im, ssm_D, w_glu, b_glu, attn_out_g,
                  ssm_out_g, w_out, norm2_g, w_up, w_down):
    def attn_perm(m):
        lead = m.shape[:-1]
        return m.reshape(lead + (N_KV_HEADS, REP, HEAD_DIM)).swapaxes(-3, -2).reshape(lead + (ATTN_W,))

    w_in_p = jnp.concatenate([attn_perm(w_in[:, :ATTN_W]), w_in[:, ATTN_W:]], axis=1).astype(BF16)
    e = jnp.kron(jnp.eye(E_TILE // HEAD_DIM, dtype=F32),
                 jnp.full((HEAD_DIM, HEAD_DIM), 1.0 / HEAD_DIM, F32)).astype(BF16)
    gqk = jnp.concatenate([jnp.tile(q_norm_g, N_Q_HEADS) * (HEAD_DIM ** -0.5 * LOG2E),
                           jnp.tile(k_norm_g, N_KV_HEADS)])[None, :]

    fill = jnp.full((N_Q_HEADS, KEY_ROWS), NEG_INF, F32).at[:, N_KEYS].set(sinks * LOG2E)

    lam_re = jnp.minimum(ssm_A_re, EIG_CLIP)
    lam_im = ssm_A_im
    dt = jnp.exp(ssm_log_dt)[:, None]
    mag = jnp.exp(lam_re * dt)
    ar = mag * jnp.cos(lam_im * dt)
    ai = mag * jnp.sin(lam_im * dt)
    den = lam_re * lam_re + lam_im * lam_im
    cr = ((ar - 1.0) * lam_re + ai * lam_im) / den
    ci = (ai * lam_re - (ar - 1.0) * lam_im) / den
    bb_re = cr[..., None] * ssm_B_re - ci[..., None] * ssm_B_im
    bb_im = cr[..., None] * ssm_B_im + ci[..., None] * ssm_B_re
    a = jnp.stack([ar.reshape(2, HALF_RI), ai.reshape(2, HALF_RI)], axis=1).reshape(4, HALF_RI)
    eye = jnp.eye(HALF_G, dtype=F32)
    bb = jnp.stack([bb_re, bb_im], 0).reshape(2, 2, HALF_G, SSM_P, SSM_CH)
    bbd = jnp.einsum('ab,rhapc->hacrbp', eye, bb).reshape(2, HALF_U, HALF_STATE).astype(BF16)
    cc = jnp.stack([ssm_C_re, -ssm_C_im], 0).reshape(2, 2, HALF_G, SSM_CH, SSM_P)
    ccat = jnp.einsum('ab,rhacp->hrapbc', eye, cc).reshape(2, HALF_STATE, HALF_U).astype(BF16)

    w_out_p = jnp.concatenate(
        [attn_perm(w_out[:ATTN_W].T).T, w_out[ATTN_W:]], axis=0).astype(BF16)
    return {
        "g1": norm1_g[None, :], "w_in": w_in_p, "e": e, "gqk": gqk, "fill": fill,
        "a": a, "bbd": bbd, "ccat": ccat, "dvec": ssm_D.reshape(1, SSM_W),
        "w_glu": w_glu.astype(BF16), "b_glu": b_glu[None, :],
        "ga": attn_perm(attn_out_g)[None, :], "gs": ssm_out_g[None, :], "w_out": w_out_p,
        "g2": norm2_g[None, :], "w_up": w_up.astype(BF16), "w_down": w_down.astype(BF16),
    }


def _state_to_lanes(re, im):
    nb = re.shape[0]
    return jnp.stack([re.reshape(nb, 2, HALF_RI), im.reshape(nb, 2, HALF_RI)], axis=2).reshape(nb, STATE_W)


def _lanes_to_state(st):
    nb = st.shape[0]
    s = st.reshape(nb, 2, 2, HALF_RI)
    return s[:, :, 0].reshape(nb, SSM_G, SSM_P), s[:, :, 1].reshape(nb, SSM_G, SSM_P)


def kernel(x_prompt, x_sample, cache_swa_k, cache_swa_v, state_ssm_re, state_ssm_im, meta_tokens, norm1_g, w_in, q_norm_g, k_norm_g, sinks, ssm_A_re, ssm_A_im, ssm_log_dt, ssm_B_re, ssm_B_im, ssm_C_re, ssm_C_im, ssm_D, w_glu, b_glu, attn_out_g, ssm_out_g, w_out, norm2_g, w_up, w_down):
    assert norm1_g.shape[0] == 1, "single-layer trunk"
    nb, seq, _ = x_prompt.shape
    nd, dseq, _ = x_sample.shape
    assert dseq == CHUNK and seq % CHUNK == 0 and seq >= WINDOW and nb % BG == 0 and nd % BG == 0
    w = _prep_weights(norm1_g[0], w_in[0], q_norm_g[0], k_norm_g[0], sinks[0], ssm_A_re[0],
                      ssm_A_im[0], ssm_log_dt[0], ssm_B_re[0], ssm_B_im[0], ssm_C_re[0],
                      ssm_C_im[0], ssm_D[0], w_glu[0], b_glu[0], attn_out_g[0], ssm_out_g[0],
                      w_out[0], norm2_g[0], w_up[0], w_down[0])

    k_meta, v_meta, st_meta = _meta_call(meta_tokens, w)
    kmeta_b = jnp.broadcast_to(k_meta[None], (nb, N_META, KV_W))
    vmeta_b = jnp.broadcast_to(v_meta[None], (nb, N_META, KV_W))
    zeros_win = jnp.zeros((nb, WINDOW, KV_W), F32)
    dp, kp, vp, stp = _mixer_call(
        x_prompt, jnp.broadcast_to(st_meta, (nb, STATE_W)), kmeta_b, vmeta_b, zeros_win, zeros_win,
        w, chunk_offset=0, n_out=WINDOW // CHUNK)
    y_prompt = _mlp_call(x_prompt.reshape(nb * seq, D_MODEL), dp.reshape(nb * seq, D_MODEL),
                         w).reshape(nb, seq, D_MODEL)

    ck = cache_swa_k[0].reshape(nd, N_META + WINDOW, KV_W)
    cv = cache_swa_v[0].reshape(nd, N_META + WINDOW, KV_W)
    ds, ks, vs, sts = _mixer_call(
        x_sample, _state_to_lanes(state_ssm_re[0], state_ssm_im[0]),
        ck[:, :N_META], cv[:, :N_META], ck[:, N_META:], cv[:, N_META:],
        w, chunk_offset=WINDOW // CHUNK, n_out=1)
    y_sample = _mlp_call(x_sample.reshape(nd * dseq, D_MODEL), ds.reshape(nd * dseq, D_MODEL),
                         w).reshape(nd, dseq, D_MODEL)

    kv5 = lambda t: t.reshape(t.shape[0], t.shape[1], N_KV_HEADS, HEAD_DIM)[None]
    new_k_prompt = kv5(jnp.concatenate([kmeta_b, kp], axis=1))
    new_v_prompt = kv5(jnp.concatenate([vmeta_b, vp], axis=1))
    srp, sip = _lanes_to_state(stp)
    srs, sis = _lanes_to_state(sts)
    return (y_prompt, y_sample, new_k_prompt, new_v_prompt, srp[None], sip[None],
            kv5(ks), kv5(vs), srs[None], sis[None])
```

```python
import functools

import jax
import jax.numpy as jnp
from jax import lax
from jax.experimental import pallas as pl
from jax.experimental.pallas import tpu as pltpu

D_MODEL = 1024
CHUNK = 64
N_META = 16
HEAD_DIM = 64
ATTN_W = 512
N_KV_HEADS = 2
REP = 4
N_Q_HEADS = N_KV_HEADS * REP
KV_W = N_KV_HEADS * HEAD_DIM
WINDOW = 128
SSM_W = 512
SSM_CH = 16
SSM_G = 32
SSM_P = 64
D_FF = 4 * D_MODEL
IN_W = ATTN_W + 2 * KV_W + SSM_W
EPS = 1e-6
DT_MIN = 1e-3
DT_MAX = 1e-1
EIG_CLIP = -1e-4
NEG_INF = -1e30

SUBLANES = 8
LANES = 128
VMEM_LIMIT_BYTES = 56 * 1024 * 1024

BG = SUBLANES
ROWS = BG * CHUNK
QK_W = ATTN_W + KV_W
V_OFF = QK_W
U_OFF = QK_W + KV_W
N_KEYS = N_META + WINDOW + CHUNK
KEY_ROWS = 256
LOG2E = 1.4426950408889634
N_SLOTS = WINDOW // CHUNK + 1
HALF_G = SSM_G // 2
HALF_RI = HALF_G * SSM_P
HALF_STATE = 2 * HALF_RI
STATE_W = 2 * HALF_STATE
HALF_U = HALF_G * SSM_CH
U_SLABS = SSM_W // LANES
SCAN_W = 512
MLP_ROWS = 512
E_TILE = 256
MXU_N = 256
FF_BLK = 1024

BF16 = jnp.bfloat16
F32 = jnp.float32

_STEP_ORDER = (
    "unpack norm GLU_OUT AN BU:0 S:0 BU:1 SCAN:0 S:1 OP:0 PV:0 S:2 OP:1 SCAN:1 PV:1 S:3 OP:2 PV:2 S:4 OP:3 "
    "PV:3 S:5 W:0 PV:4 S:6 W:1 PV:5 S:7 W:2 PV:6 QKN C:0 PV:7 W:3 C:1 W:4 GLU_IN")


def _rms(x, g):
    ms = jnp.mean(x * x, axis=-1, keepdims=True)
    return x * lax.rsqrt(ms + EPS) * g


def _project(x, g1_ref, win_ref, e_ref, gqk_ref):
    xn = _rms(x, g1_ref[...]).astype(BF16)
    z = jnp.dot(xn, win_ref[...], preferred_element_type=F32)
    zqk = z[:, :QK_W]
    sq = (zqk * zqk).astype(BF16)
    ms = jnp.concatenate(
        [jnp.dot(sq[:, o:o + n], e_ref[:n, :n], preferred_element_type=F32)
         for o, n in ((0, E_TILE), (E_TILE, E_TILE), (2 * E_TILE, QK_W - 2 * E_TILE))], axis=1)
    qk = zqk * lax.rsqrt(ms + EPS) * gqk_ref[...]
    return qk, z


def _meta_kernel(meta_ref, g1_ref, win_ref, e_ref, gqk_ref, a_ref, bbd_ref,
                 k_ref, v_ref, st_ref, bu_scr):
    qk, z = _project(meta_ref[...], g1_ref, win_ref, e_ref, gqk_ref)
    k_ref[...] = qk[:, ATTN_W:]
    v_ref[...] = z[:, V_OFF:U_OFF]
    u = z[:, U_OFF:].astype(BF16)
    for h in range(2):
        bu_scr[:, h * HALF_STATE:(h + 1) * HALF_STATE] = jnp.dot(
            u[:, h * HALF_U:(h + 1) * HALF_U], bbd_ref[h], preferred_element_type=F32)
    for h in range(2):
        ar = a_ref[2 * h:2 * h + 1, :]
        ai = a_ref[2 * h + 1:2 * h + 2, :]
        cr = h * HALF_STATE
        ci = cr + HALF_RI
        xr = jnp.zeros((1, HALF_RI), F32)
        xi = jnp.zeros((1, HALF_RI), F32)
        for t in range(N_META):
            br = bu_scr[t:t + 1, cr:cr + HALF_RI]
            bi = bu_scr[t:t + 1, ci:ci + HALF_RI]
            xr, xi = ar * xr - ai * xi + br, ar * xi + ai * xr + bi
        st_ref[:, cr:cr + HALF_RI] = xr
        st_ref[:, ci:ci + HALF_RI] = xi


def _mixer_kernel(x_ref, st0_ref, kmeta_ref, vmeta_ref, kprev_ref, vprev_ref,
                  g1_ref, win_ref, e_ref, gqk_ref, fill_ref, row0_ref, a_ref, bbd_ref, ccat_ref, dvec_ref,
                  wglu_ref, bglu_ref, ga_ref, gs_ref, wout_ref,
                  d_ref, kout_ref, vout_ref, st_ref,
                  z_scr, xn_scr, qs_scr, du_scr, attn_scr, utb_scr, bu_scr, ytb_scr, ybt_scr, kwin, vwin,
                  merged_scr, *, n_chunks, n_bg, chunk_offset, n_out):
    s = pl.program_id(0)
    n_steps = n_bg * n_chunks
    c = _mix_chunk(s, n_steps) % n_chunks
    live = jnp.logical_and(s >= 1, s <= n_steps)
    gc = c + chunk_offset
    row0 = row0_ref[0]

    @pl.when(s == 0)
    def _():
        z_scr[...] = jnp.zeros((ROWS, IN_W), F32)
        ybt_scr[...] = jnp.zeros((ROWS, SSM_W), F32)
        attn_scr[...] = jnp.zeros((ROWS, ATTN_W), F32)

    @pl.when(jnp.logical_and(c == 0, s <= n_steps))
    def _():
        st_ref[...] = st0_ref[...]
        kwin[:, 0:N_META, :] = kmeta_ref[...].astype(BF16)
        vwin[:, 0:N_META, :] = vmeta_ref[...].astype(BF16)
        kwin[:, N_META:N_META + WINDOW, :] = kprev_ref[...].astype(BF16)
        vwin[:, N_META:N_META + WINDOW, :] = vprev_ref[...].astype(BF16)
        kwin[:, N_META + WINDOW:, :] = jnp.zeros((BG, KEY_ROWS - N_META - WINDOW, KV_W), BF16)
        vwin[:, N_META + WINDOW:, :] = jnp.zeros((BG, KEY_ROWS - N_META - WINDOW, KV_W), BF16)

    lane = lax.broadcasted_iota(jnp.int32, (CHUNK, LANES), 1)
    head_lanes = (lane < HEAD_DIM, lane >= HEAD_DIM)
    key_limit = jnp.where(gc >= N_SLOTS - 1, N_KEYS, N_META + CHUNK * (gc + 1))
    key_ok = lax.broadcasted_iota(jnp.int32, (1, KEY_ROWS), 1) < key_limit

    def unpack():
        k = z_scr[:, ATTN_W:QK_W].reshape(BG, CHUNK, KV_W)
        v = z_scr[:, V_OFF:U_OFF].reshape(BG, CHUNK, KV_W)
        u = z_scr[:, U_OFF:]
        slot_row = pl.multiple_of(N_META + lax.rem(gc, N_SLOTS) * CHUNK, 16)
        kwin[:, pl.ds(slot_row, CHUNK), :] = k.astype(BF16)
        vwin[:, pl.ds(slot_row, CHUNK), :] = v.astype(BF16)
        out_row = pl.multiple_of(lax.rem(c, n_out) * CHUNK, CHUNK)
        kout_ref[:, pl.ds(out_row, CHUNK), :] = k
        vout_ref[:, pl.ds(out_row, CHUNK), :] = v
        for b in range(BG):
            qb = z_scr[b * CHUNK:(b + 1) * CHUNK, 0:ATTN_W]
            parts = [jnp.where(head_lanes[kvh], qb[:, r * LANES:(r + 1) * LANES], 0.0)
                     for kvh in range(N_KV_HEADS) for r in range(REP)]
            qs_scr[b] = jnp.concatenate(parts, axis=0).astype(BF16)
        for b in range(BG):
            for j in range(U_SLABS):
                utb_scr[j, pl.ds(b, CHUNK, stride=BG), :] = (
                    u[b * CHUNK:(b + 1) * CHUNK, j * LANES:(j + 1) * LANES])
        du_scr[...] = dvec_ref[...] * u

    def out_proj(j):
        cols = slice(j * MXU_N, (j + 1) * MXU_N)
        d_ref[:, :, cols] = jnp.dot(merged_scr[...], wout_ref[:, cols],
                                    preferred_element_type=F32).reshape(BG, CHUNK, MXU_N)

    def proj_norm():
        xn_scr[...] = _rms(x_ref[...].reshape(ROWS, D_MODEL), g1_ref[...]).astype(BF16)

    def proj(j):
        cols = slice(j * MXU_N, (j + 1) * MXU_N)
        z_scr[:, cols] = jnp.dot(xn_scr[...], win_ref[:, cols], preferred_element_type=F32)

    def proj_qknorm():
        for o, n in ((0, E_TILE), (E_TILE, E_TILE), (2 * E_TILE, QK_W - 2 * E_TILE)):
            zq = z_scr[:, o:o + n]
            ms = jnp.dot((zq * zq).astype(BF16), e_ref[:n, :n], preferred_element_type=F32)
            z_scr[:, o:o + n] = zq * lax.rsqrt(ms + EPS) * gqk_ref[:, o:o + n]

    probs = {}

    def attn_scores(b):
        sc = lax.dot_general(qs_scr[b], kwin[b], (((1,), (1,)), ((), ())), preferred_element_type=F32)
        p_parts = []
        inv_l = []
        for i in range(N_Q_HEADS):
            sb = jnp.where(key_ok, sc[i * CHUNK:(i + 1) * CHUNK], fill_ref[i:i + 1, :])
            m = jnp.max(sb, axis=-1, keepdims=True)
            p = jnp.exp2(sb - m)
            l = jnp.sum(p, axis=-1, keepdims=True)
            p_parts.append(p.astype(BF16))
            inv_l.append(1.0 / l)
        probs[b] = (jnp.concatenate(p_parts, axis=0), inv_l)

    def attn_values(b):
        p_all, inv_l = probs.pop(b)
        o = jnp.dot(p_all, vwin[b], preferred_element_type=F32)
        outs = []
        for r in range(REP):
            o0 = o[r * CHUNK:(r + 1) * CHUNK] * inv_l[r]
            o1 = o[(REP + r) * CHUNK:(REP + r + 1) * CHUNK] * inv_l[REP + r]
            outs.append(jnp.where(head_lanes[0], o0, o1))
        attn_scr[b * CHUNK:(b + 1) * CHUNK, :] = jnp.concatenate(outs, axis=1)

    def s5_in(h):
        uh = jnp.concatenate([utb_scr[2 * h], utb_scr[2 * h + 1]], axis=1).astype(BF16)
        bu_scr[:, h * HALF_STATE:(h + 1) * HALF_STATE] = jnp.dot(
            uh, bbd_ref[h], preferred_element_type=F32)

    def s5_scan(h):
        for w in range(HALF_RI // SCAN_W):
            cr = h * HALF_STATE + w * SCAN_W
            ci = cr + HALF_RI
            ar = jnp.broadcast_to(a_ref[2 * h:2 * h + 1, w * SCAN_W:(w + 1) * SCAN_W], (BG, SCAN_W))
            ai = jnp.broadcast_to(a_ref[2 * h + 1:2 * h + 2, w * SCAN_W:(w + 1) * SCAN_W], (BG, SCAN_W))

            def scan_body(t, carry, cr=cr, ci=ci, ar=ar, ai=ai):
                xr, xi = carry
                r = pl.multiple_of(t * BG, BG)
                rl = pl.multiple_of(row0 + t * BG, BG)
                br = bu_scr[pl.ds(rl, BG), cr:cr + SCAN_W]
                bi = bu_scr[pl.ds(rl, BG), ci:ci + SCAN_W]
                nr = ar * xr - ai * xi + br
                ni = ar * xi + ai * xr + bi
                bu_scr[pl.ds(r, BG), cr:cr + SCAN_W] = nr
                bu_scr[pl.ds(r, BG), ci:ci + SCAN_W] = ni
                return nr, ni

            xr0 = st_ref[:, cr:cr + SCAN_W]
            xi0 = st_ref[:, ci:ci + SCAN_W]
            xr, xi = lax.fori_loop(0, CHUNK, scan_body, (xr0, xi0), unroll=True)
            st_ref[:, cr:cr + SCAN_W] = jnp.where(live, xr, xr0)
            st_ref[:, ci:ci + SCAN_W] = jnp.where(live, xi, xi0)

    def s5_out(h):
        xs = bu_scr[pl.ds(pl.multiple_of(row0, ROWS), ROWS), h * HALF_STATE:(h + 1) * HALF_STATE]
        yh = jnp.dot(xs.astype(BF16), ccat_ref[h],
                     preferred_element_type=F32)
        for jj in range(2):
            ytb_scr[2 * h + jj] = yh[:, jj * LANES:(jj + 1) * LANES]

    def glu_in():
        for b in range(BG):
            for j in range(U_SLABS):
                ybt_scr[b * CHUNK:(b + 1) * CHUNK, j * LANES:(j + 1) * LANES] = (
                    ytb_scr[j, pl.ds(b, CHUNK, stride=BG), :])
        ybt_scr[...] = jax.nn.gelu(ybt_scr[...] + du_scr[...])

    def glu_out():
        zs = ybt_scr[...]
        gate = jnp.dot(zs.astype(BF16), wglu_ref[...], preferred_element_type=F32) + bglu_ref[...]
        s_out = zs * jax.nn.sigmoid(gate)
        merged_scr[:, ATTN_W:] = _rms(s_out, gs_ref[...]).astype(BF16)

    def attn_norm():
        merged_scr[:, :ATTN_W] = _rms(attn_scr[...], ga_ref[...]).astype(BF16)

    pieces = {"unpack": unpack, "norm": proj_norm, "W": proj, "QKN": proj_qknorm, "OP": out_proj,
              "S": attn_scores, "PV": attn_values, "BU": s5_in, "SCAN": s5_scan, "C": s5_out,
              "GLU_IN": glu_in, "GLU_OUT": glu_out, "AN": attn_norm}
    for item in _STEP_ORDER.split():
        name, _, arg = item.partition(":")
        pieces[name](*([int(arg)] if arg else []))


def _mlp_kernel(x_ref, d_ref, g2_ref, wup_ref, wdown_ref, o_ref):
    h = x_ref[...] + d_ref[...]
    xn = _rms(h, g2_ref[...]).astype(BF16)
    acc = h
    for j in range(D_FF // FF_BLK):
        f = jnp.dot(xn, wup_ref[:, j * FF_BLK:(j + 1) * FF_BLK], preferred_element_type=F32)
        f = jnp.maximum(f, 0.0)
        acc = acc + jnp.dot((f * f).astype(BF16), wdown_ref[j * FF_BLK:(j + 1) * FF_BLK, :],
                            preferred_element_type=F32)
    o_ref[...] = acc


def _mix_chunk(s, n_steps):
    return jnp.clip(s - 1, 0, n_steps - 1)


def _const_spec(shape):
    n = len(shape)
    return pl.BlockSpec(shape, lambda *_: (0,) * n, pipeline_mode=pl.Buffered(1))


def _meta_call(meta, w):
    return pl.pallas_call(
        _meta_kernel,
        out_shape=(jax.ShapeDtypeStruct((N_META, KV_W), F32),
                   jax.ShapeDtypeStruct((N_META, KV_W), F32),
                   jax.ShapeDtypeStruct((1, STATE_W), F32)),
        scratch_shapes=[pltpu.VMEM((N_META, STATE_W), F32)],
        compiler_params=pltpu.CompilerParams(vmem_limit_bytes=VMEM_LIMIT_BYTES),
        name="meta",
    )(meta, w["g1"], w["w_in"], w["e"], w["gqk"], w["a"], w["bbd"])


def _mixer_call(x, st0, kmeta, vmeta, kprev, vprev, w, *, chunk_offset, n_out):
    nb, s_len, _ = x.shape
    n_chunks = s_len // CHUNK
    n_bg = nb // BG
    n_steps = n_bg * n_chunks
    assert n_chunks % n_out == 0
    grid = (n_steps + 2,)

    def chunk_block(gidx):
        return (gidx // n_chunks, gidx % n_chunks, 0)

    x_index = lambda s: chunk_block(jnp.minimum(s, n_steps - 1))
    d_index = lambda s: chunk_block(jnp.clip(s - 2, 0, n_steps - 1))
    per_bg = lambda s: (_mix_chunk(s, n_steps) // n_chunks, 0, 0)
    per_bg2 = lambda s: (_mix_chunk(s, n_steps) // n_chunks, 0)
    in_specs = [
        pl.BlockSpec((BG, CHUNK, D_MODEL), x_index),
        pl.BlockSpec((BG, STATE_W), per_bg2),
        pl.BlockSpec((BG, N_META, KV_W), per_bg),
        pl.BlockSpec((BG, N_META, KV_W), per_bg),
        pl.BlockSpec((BG, WINDOW, KV_W), per_bg),
        pl.BlockSpec((BG, WINDOW, KV_W), per_bg),
        _const_spec((1, D_MODEL)),
        _const_spec((D_MODEL, IN_W)),
        _const_spec((E_TILE, E_TILE)),
        _const_spec((1, QK_W)),
        _const_spec((N_Q_HEADS, KEY_ROWS)),
        pl.BlockSpec(memory_space=pltpu.SMEM),
        _const_spec((4, HALF_RI)),
        _const_spec((2, HALF_U, HALF_STATE)),
        _const_spec((2, HALF_STATE, HALF_U)),
        _const_spec((1, SSM_W)),
        _const_spec((SSM_W, SSM_W)),
        _const_spec((1, SSM_W)),
        _const_spec((1, ATTN_W)),
        _const_spec((1, SSM_W)),
        _const_spec((D_MODEL, D_MODEL)),
    ]
    out_specs = (
        pl.BlockSpec((BG, CHUNK, D_MODEL), d_index),
        pl.BlockSpec((BG, n_out * CHUNK, KV_W), per_bg),
        pl.BlockSpec((BG, n_out * CHUNK, KV_W), per_bg),
        pl.BlockSpec((BG, STATE_W), per_bg2),
    )
    out_shape = (
        jax.ShapeDtypeStruct((nb, s_len, D_MODEL), F32),
        jax.ShapeDtypeStruct((nb, n_out * CHUNK, KV_W), F32),
        jax.ShapeDtypeStruct((nb, n_out * CHUNK, KV_W), F32),
        jax.ShapeDtypeStruct((nb, STATE_W), F32),
    )
    scratch = [
        pltpu.VMEM((ROWS, IN_W), F32),
        pltpu.VMEM((ROWS, D_MODEL), BF16),
        pltpu.VMEM((BG, N_Q_HEADS * CHUNK, LANES), BF16),
        pltpu.VMEM((ROWS, SSM_W), F32),
        pltpu.VMEM((ROWS, ATTN_W), F32),
        pltpu.VMEM((U_SLABS, ROWS, LANES), F32),
        pltpu.VMEM((ROWS, STATE_W), F32),
        pltpu.VMEM((U_SLABS, ROWS, LANES), F32),
        pltpu.VMEM((ROWS, SSM_W), F32),
        pltpu.VMEM((BG, KEY_ROWS, KV_W), BF16),
        pltpu.VMEM((BG, KEY_ROWS, KV_W), BF16),
        pltpu.VMEM((ROWS, D_MODEL), BF16),
    ]
    kern = functools.partial(_mixer_kernel, n_chunks=n_chunks, n_bg=n_bg,
                             chunk_offset=chunk_offset, n_out=n_out)
    return pl.pallas_call(
        kern, grid=grid, in_specs=in_specs, out_specs=out_specs, out_shape=out_shape,
        scratch_shapes=scratch,
        compiler_params=pltpu.CompilerParams(
            dimension_semantics=("arbitrary",), vmem_limit_bytes=VMEM_LIMIT_BYTES),
        name="mixer",
    )(x, st0, kmeta, vmeta, kprev, vprev,
      w["g1"], w["w_in"], w["e"], w["gqk"], w["fill"], jnp.zeros((1,), jnp.int32),
      w["a"], w["bbd"], w["ccat"], w["dvec"],
      w["w_glu"], w["b_glu"], w["ga"], w["gs"], w["w_out"])


def _mlp_call(x, d, w):
    n = x.shape[0]
    return pl.pallas_call(
        _mlp_kernel,
        grid=(n // MLP_ROWS,),
        in_specs=[pl.BlockSpec((MLP_ROWS, D_MODEL), lambda i: (i, 0)),
                  pl.BlockSpec((MLP_ROWS, D_MODEL), lambda i: (i, 0)),
                  _const_spec((1, D_MODEL)),
                  _const_spec((D_MODEL, D_FF)),
                  _const_spec((D_FF, D_MODEL))],
        out_specs=pl.BlockSpec((MLP_ROWS, D_MODEL), lambda i: (i, 0)),
        out_shape=jax.ShapeDtypeStruct((n, D_MODEL), F32),
        compiler_params=pltpu.CompilerParams(
            dimension_semantics=("arbitrary",), vmem_limit_bytes=VMEM_LIMIT_BYTES),
        name="mlp",
    )(x, d, w["g2"], w["w_up"], w["w_down"])


def _prep_weights(norm1_g, w_in, q_norm_g, k_norm_g, sinks, ssm_A_re, ssm_A_im, ssm_log_dt,
                  ssm_B_re, ssm_B_im, ssm_C_re, ssm_C_im, ssm_D, w_glu, b_glu, attn_out_g,
                  ssm_out_g, w_out, norm2_g, w_up, w_down):
    def attn_perm(m):
        lead = m.shape[:-1]
        return m.reshape(lead + (N_KV_HEADS, REP, HEAD_DIM)).swapaxes(-3, -2).reshape(lead + (ATTN_W,))

    w_in_p = jnp.concatenate([attn_perm(w_in[:, :ATTN_W]), w_in[:, ATTN_W:]], axis=1).astype(BF16)
    e = jnp.kron(jnp.eye(E_TILE // HEAD_DIM, dtype=F32),
                 jnp.full((HEAD_DIM, HEAD_DIM), 1.0 / HEAD_DIM, F32)).astype(BF16)
    gqk = jnp.concatenate([jnp.tile(q_norm_g, N_Q_HEADS) * (HEAD_DIM ** -0.5 * LOG2E),
                           jnp.tile(k_norm_g, N_KV_HEADS)])[None, :]

    fill = jnp.full((N_Q_HEADS, KEY_ROWS), NEG_INF, F32).at[:, N_KEYS].set(sinks * LOG2E)

    lam_re = jnp.minimum(ssm_A_re, EIG_CLIP)
    lam_im = ssm_A_im
    dt = jnp.exp(ssm_log_dt)[:, None]
    mag = jnp.exp(lam_re * dt)
    ar = mag * jnp.cos(lam_im * dt)
    ai = mag * jnp.sin(lam_im * dt)
    den = lam_re * lam_re + lam_im * lam_im
    cr = ((ar - 1.0) * lam_re + ai * lam_im) / den
    ci = (ai * lam_re - (ar - 1.0) * lam_im) / den
    bb_re = cr[..., None] * ssm_B_re - ci[..., None] * ssm_B_im
    bb_im = cr[..., None] * ssm_B_im + ci[..., None] * ssm_B_re
    a = jnp.stack([ar.reshape(2, HALF_RI), ai.reshape(2, HALF_RI)], axis=1).reshape(4, HALF_RI)
    eye = jnp.eye(HALF_G, dtype=F32)
    bb = jnp.stack([bb_re, bb_im], 0).reshape(2, 2, HALF_G, SSM_P, SSM_CH)
    bbd = jnp.einsum('ab,rhapc->hacrbp', eye, bb).reshape(2, HALF_U, HALF_STATE).astype(BF16)
    cc = jnp.stack([ssm_C_re, -ssm_C_im], 0).reshape(2, 2, HALF_G, SSM_CH, SSM_P)
    ccat = jnp.einsum('ab,rhacp->hrapbc', eye, cc).reshape(2, HALF_STATE, HALF_U).astype(BF16)

    w_out_p = jnp.concatenate(
        [attn_perm(w_out[:ATTN_W].T).T, w_out[ATTN_W:]], axis=0).astype(BF16)
    return {
        "g1": norm1_g[None, :], "w_in": w_in_p, "e": e, "gqk": gqk, "fill": fill,
        "a": a, "bbd": bbd, "ccat": ccat, "dvec": ssm_D.reshape(1, SSM_W),
        "w_glu": w_glu.astype(BF16), "b_glu": b_glu[None, :],
        "ga": attn_perm(attn_out_g)[None, :], "gs": ssm_out_g[None, :], "w_out": w_out_p,
        "g2": norm2_g[None, :], "w_up": w_up.astype(BF16), "w_down": w_down.astype(BF16),
    }


def _state_to_lanes(re, im):
    nb = re.shape[0]
    return jnp.stack([re.reshape(nb, 2, HALF_RI), im.reshape(nb, 2, HALF_RI)], axis=2).reshape(nb, STATE_W)


def _lanes_to_state(st):
    nb = st.shape[0]
    s = st.reshape(nb, 2, 2, HALF_RI)
    return s[:, :, 0].reshape(nb, SSM_G, SSM_P), s[:, :, 1].reshape(nb, SSM_G, SSM_P)


def kernel(x_prompt, x_sample, cache_swa_k, cache_swa_v, state_ssm_re, state_ssm_im, meta_tokens, norm1_g, w_in, q_norm_g, k_norm_g, sinks, ssm_A_re, ssm_A_im, ssm_log_dt, ssm_B_re, ssm_B_im, ssm_C_re, ssm_C_im, ssm_D, w_glu, b_glu, attn_out_g, ssm_out_g, w_out, norm2_g, w_up, w_down):
    assert norm1_g.shape[0] == 1, "single-layer trunk"
    nb, seq, _ = x_prompt.shape
    nd, dseq, _ = x_sample.shape
    assert dseq == CHUNK and seq % CHUNK == 0 and seq >= WINDOW and nb % BG == 0 and nd % BG == 0
    w = _prep_weights(norm1_g[0], w_in[0], q_norm_g[0], k_norm_g[0], sinks[0], ssm_A_re[0],
                      ssm_A_im[0], ssm_log_dt[0], ssm_B_re[0], ssm_B_im[0], ssm_C_re[0],
                      ssm_C_im[0], ssm_D[0], w_glu[0], b_glu[0], attn_out_g[0], ssm_out_g[0],
                      w_out[0], norm2_g[0], w_up[0], w_down[0])

    k_meta, v_meta, st_meta = _meta_call(meta_tokens, w)
    kmeta_b = jnp.broadcast_to(k_meta[None], (nb, N_META, KV_W))
    vmeta_b = jnp.broadcast_to(v_meta[None], (nb, N_META, KV_W))
    zeros_win = jnp.zeros((nb, WINDOW, KV_W), F32)
    dp, kp, vp, stp = _mixer_call(
        x_prompt, jnp.broadcast_to(st_meta, (nb, STATE_W)), kmeta_b, vmeta_b, zeros_win, zeros_win,
        w, chunk_offset=0, n_out=WINDOW // CHUNK)
    y_prompt = _mlp_call(x_prompt.reshape(nb * seq, D_MODEL), dp.reshape(nb * seq, D_MODEL),
                         w).reshape(nb, seq, D_MODEL)

    ck = cache_swa_k[0].reshape(nd, N_META + WINDOW, KV_W)
    cv = cache_swa_v[0].reshape(nd, N_META + WINDOW, KV_W)
    ds, ks, vs, sts = _mixer_call(
        x_sample, _state_to_lanes(state_ssm_re[0], state_ssm_im[0]),
        ck[:, :N_META], cv[:, :N_META], ck[:, N_META:], cv[:, N_META:],
        w, chunk_offset=WINDOW // CHUNK, n_out=1)
    y_sample = _mlp_call(x_sample.reshape(nd * dseq, D_MODEL), ds.reshape(nd * dseq, D_MODEL),
                         w).reshape(nd, dseq, D_MODEL)

    kv5 = lambda t: t.reshape(t.shape[0], t.shape[1], N_KV_HEADS, HEAD_DIM)[None]
    new_k_prompt = kv5(jnp.concatenate([kmeta_b, kp], axis=1))
    new_v_prompt = kv5(jnp.concatenate([vmeta_b, vp], axis=1))
    srp, sip = _lanes_to_state(stp)
    srs, sis = _lanes_to_state(sts)
    return (y_prompt, y_sample, new_k_prompt, new_v_prompt, srp[None], sip[None],
            kv5(ks), kv5(vs), srs[None], sis[None])
```

```python
import functools

import jax
import jax.numpy as jnp
from jax import lax
from jax.experimental import pallas as pl
from jax.experimental.pallas import tpu as pltpu

D_MODEL = 1024
CHUNK = 64
N_META = 16
HEAD_DIM = 64
ATTN_W = 512
N_KV_HEADS = 2
REP = 4
N_Q_HEADS = N_KV_HEADS * REP
KV_W = N_KV_HEADS * HEAD_DIM
WINDOW = 128
SSM_W = 512
SSM_CH = 16
SSM_G = 32
SSM_P = 64
D_FF = 4 * D_MODEL
IN_W = ATTN_W + 2 * KV_W + SSM_W
EPS = 1e-6
DT_MIN = 1e-3
DT_MAX = 1e-1
EIG_CLIP = -1e-4
NEG_INF = -1e30

SUBLANES = 8
LANES = 128
VMEM_LIMIT_BYTES = 56 * 1024 * 1024

BG = SUBLANES
ROWS = BG * CHUNK
QK_W = ATTN_W + KV_W
V_OFF = QK_W
U_OFF = QK_W + KV_W
N_KEYS = N_META + WINDOW + CHUNK
KEY_ROWS = 256
LOG2E = 1.4426950408889634
N_SLOTS = WINDOW // CHUNK + 1
HALF_G = SSM_G // 2
HALF_RI = HALF_G * SSM_P
HALF_STATE = 2 * HALF_RI
STATE_W = 2 * HALF_STATE
HALF_U = HALF_G * SSM_CH
U_SLABS = SSM_W // LANES
SCAN_W = 512
MLP_ROWS = 1024
MLP_SUB_ROWS = 512
E_TILE = 256
MXU_N = 256
FF_BLK = 1024

BF16 = jnp.bfloat16
F32 = jnp.float32

_STEP_ORDER = (
    "unpack norm GLU_OUT AN BU:0 S:0 BU:1 SCAN:0 S:1 OP:0 PV:0 S:2 OP:1 SCAN:1 PV:1 S:3 OP:2 PV:2 S:4 OP:3 "
    "PV:3 S:5 W:0 PV:4 S:6 W:1 PV:5 S:7 W:2 PV:6 QKN C:0 PV:7 W:3 C:1 W:4 GLU_IN")


def _rms(x, g):
    ms = jnp.mean(x * x, axis=-1, keepdims=True)
    return x * lax.rsqrt(ms + EPS) * g


def _project(x, g1_ref, win_ref, e_ref, gqk_ref):
    xn = _rms(x, g1_ref[...]).astype(BF16)
    z = jnp.dot(xn, win_ref[...], preferred_element_type=F32)
    zqk = z[:, :QK_W]
    sq = (zqk * zqk).astype(BF16)
    ms = jnp.concatenate(
        [jnp.dot(sq[:, o:o + n], e_ref[:n, :n], preferred_element_type=F32)
         for o, n in ((0, E_TILE), (E_TILE, E_TILE), (2 * E_TILE, QK_W - 2 * E_TILE))], axis=1)
    qk = zqk * lax.rsqrt(ms + EPS) * gqk_ref[...]
    return qk, z


def _meta_kernel(meta_ref, g1_ref, win_ref, e_ref, gqk_ref, a_ref, bbd_ref,
                 k_ref, v_ref, st_ref, bu_scr):
    qk, z = _project(meta_ref[...], g1_ref, win_ref, e_ref, gqk_ref)
    k_ref[...] = qk[:, ATTN_W:]
    v_ref[...] = z[:, V_OFF:U_OFF]
    u = z[:, U_OFF:].astype(BF16)
    for h in range(2):
        bu_scr[:, h * HALF_STATE:(h + 1) * HALF_STATE] = jnp.dot(
            u[:, h * HALF_U:(h + 1) * HALF_U], bbd_ref[h], preferred_element_type=F32)
    for h in range(2):
        ar = a_ref[2 * h:2 * h + 1, :]
        ai = a_ref[2 * h + 1:2 * h + 2, :]
        cr = h * HALF_STATE
        ci = cr + HALF_RI
        xr = jnp.zeros((1, HALF_RI), F32)
        xi = jnp.zeros((1, HALF_RI), F32)
        for t in range(N_META):
            br = bu_scr[t:t + 1, cr:cr + HALF_RI]
            bi = bu_scr[t:t + 1, ci:ci + HALF_RI]
            xr, xi = ar * xr - ai * xi + br, ar * xi + ai * xr + bi
        st_ref[:, cr:cr + HALF_RI] = xr
        st_ref[:, ci:ci + HALF_RI] = xi


def _mixer_kernel(x_ref, st0_ref, k0_ref, v0_ref,
                  g1_ref, win_ref, e_ref, gqk_ref, fill_ref, row0_ref, a_ref, bbd_ref, ccat_ref, dvec_ref,
                  wglu_ref, bglu_ref, ga_ref, gs_ref, wout_ref,
                  d_ref, kout_ref, vout_ref, st_ref,
                  z_scr, xn_scr, qs_scr, du_scr, attn_scr, utb_scr, bu_scr, ytb_scr, ybt_scr, kwin, vwin,
                  merged_scr, *, n_chunks, n_bg, chunk_offset, n_out):
    s = pl.program_id(0)
    n_steps = n_bg * n_chunks
    c = _mix_chunk(s, n_steps) % n_chunks
    live = jnp.logical_and(s >= 1, s <= n_steps)
    gc = c + chunk_offset
    row0 = row0_ref[0]

    @pl.when(s == 0)
    def _():
        z_scr[...] = jnp.zeros((ROWS, IN_W), F32)
        ybt_scr[...] = jnp.zeros((ROWS, SSM_W), F32)
        attn_scr[...] = jnp.zeros((ROWS, ATTN_W), F32)

    @pl.when(jnp.logical_and(c == 0, s <= n_steps))
    def _():
        st_ref[...] = st0_ref[...]
        kwin[:, 0:N_META + WINDOW, :] = k0_ref[...].astype(BF16)
        vwin[:, 0:N_META + WINDOW, :] = v0_ref[...].astype(BF16)
        kwin[:, N_META + WINDOW:, :] = jnp.zeros((BG, KEY_ROWS - N_META - WINDOW, KV_W), BF16)
        vwin[:, N_META + WINDOW:, :] = jnp.zeros((BG, KEY_ROWS - N_META - WINDOW, KV_W), BF16)

    lane = lax.broadcasted_iota(jnp.int32, (CHUNK, LANES), 1)
    head_lanes = (lane < HEAD_DIM, lane >= HEAD_DIM)
    key_limit = jnp.where(gc >= N_SLOTS - 1, N_KEYS, N_META + CHUNK * (gc + 1))
    key_ok = lax.broadcasted_iota(jnp.int32, (1, KEY_ROWS), 1) < key_limit

    def unpack():
        k = z_scr[:, ATTN_W:QK_W].reshape(BG, CHUNK, KV_W)
        v = z_scr[:, V_OFF:U_OFF].reshape(BG, CHUNK, KV_W)
        u = z_scr[:, U_OFF:]
        slot_row = pl.multiple_of(N_META + lax.rem(gc, N_SLOTS) * CHUNK, 16)
        kwin[:, pl.ds(slot_row, CHUNK), :] = k.astype(BF16)
        vwin[:, pl.ds(slot_row, CHUNK), :] = v.astype(BF16)
        out_row = pl.multiple_of(lax.rem(c, n_out) * CHUNK, CHUNK)
        kout_ref[:, pl.ds(out_row, CHUNK), :] = k
        vout_ref[:, pl.ds(out_row, CHUNK), :] = v
        for b in range(BG):
            qb = z_scr[b * CHUNK:(b + 1) * CHUNK, 0:ATTN_W]
            parts = [jnp.where(head_lanes[kvh], qb[:, r * LANES:(r + 1) * LANES], 0.0)
                     for kvh in range(N_KV_HEADS) for r in range(REP)]
            qs_scr[b] = jnp.concatenate(parts, axis=0).astype(BF16)
        for b in range(BG):
            for j in range(U_SLABS):
                utb_scr[j, pl.ds(b, CHUNK, stride=BG), :] = (
                    u[b * CHUNK:(b + 1) * CHUNK, j * LANES:(j + 1) * LANES])
        du_scr[...] = dvec_ref[...] * u

    def out_proj(j):
        cols = slice(j * MXU_N, (j + 1) * MXU_N)
        d_ref[:, :, cols] = jnp.dot(merged_scr[...], wout_ref[:, cols],
                                    preferred_element_type=F32).reshape(BG, CHUNK, MXU_N)

    def proj_norm():
        xn_scr[...] = _rms(x_ref[...].reshape(ROWS, D_MODEL), g1_ref[...]).astype(BF16)

    def proj(j):
        cols = slice(j * MXU_N, (j + 1) * MXU_N)
        z_scr[:, cols] = jnp.dot(xn_scr[...], win_ref[:, cols], preferred_element_type=F32)

    def proj_qknorm():
        for o, n in ((0, E_TILE), (E_TILE, E_TILE), (2 * E_TILE, QK_W - 2 * E_TILE)):
            zq = z_scr[:, o:o + n]
            ms = jnp.dot((zq * zq).astype(BF16), e_ref[:n, :n], preferred_element_type=F32)
            z_scr[:, o:o + n] = zq * lax.rsqrt(ms + EPS) * gqk_ref[:, o:o + n]

    probs = {}

    def attn_scores(b):
        sc = lax.dot_general(qs_scr[b], kwin[b], (((1,), (1,)), ((), ())), preferred_element_type=F32)
        p_parts = []
        inv_l = []
        for i in range(N_Q_HEADS):
            sb = jnp.where(key_ok, sc[i * CHUNK:(i + 1) * CHUNK], fill_ref[i:i + 1, :])
            m = jnp.max(sb, axis=-1, keepdims=True)
            p = jnp.exp2(sb - m)
            l = jnp.sum(p, axis=-1, keepdims=True)
            p_parts.append(p.astype(BF16))
            inv_l.append(1.0 / l)
        probs[b] = (jnp.concatenate(p_parts, axis=0), inv_l)

    def attn_values(b):
        p_all, inv_l = probs.pop(b)
        o = jnp.dot(p_all, vwin[b], preferred_element_type=F32)
        outs = []
        for r in range(REP):
            o0 = o[r * CHUNK:(r + 1) * CHUNK] * inv_l[r]
            o1 = o[(REP + r) * CHUNK:(REP + r + 1) * CHUNK] * inv_l[REP + r]
            outs.append(jnp.where(head_lanes[0], o0, o1))
        attn_scr[b * CHUNK:(b + 1) * CHUNK, :] = jnp.concatenate(outs, axis=1)

    def s5_in(h):
        uh = jnp.concatenate([utb_scr[2 * h], utb_scr[2 * h + 1]], axis=1).astype(BF16)
        bu_scr[:, h * HALF_STATE:(h + 1) * HALF_STATE] = jnp.dot(
            uh, bbd_ref[h], preferred_element_type=F32)

    def s5_scan(h):
        for w in range(HALF_RI // SCAN_W):
            cr = h * HALF_STATE + w * SCAN_W
            ci = cr + HALF_RI
            ar = jnp.broadcast_to(a_ref[2 * h:2 * h + 1, w * SCAN_W:(w + 1) * SCAN_W], (BG, SCAN_W))
            ai = jnp.broadcast_to(a_ref[2 * h + 1:2 * h + 2, w * SCAN_W:(w + 1) * SCAN_W], (BG, SCAN_W))

            def scan_body(t, carry, cr=cr, ci=ci, ar=ar, ai=ai):
                xr, xi = carry
                r = pl.multiple_of(t * BG, BG)
                rl = pl.multiple_of(row0 + t * BG, BG)
                br = bu_scr[pl.ds(rl, BG), cr:cr + SCAN_W]
                bi = bu_scr[pl.ds(rl, BG), ci:ci + SCAN_W]
                nr = ar * xr - ai * xi + br
                ni = ar * xi + ai * xr + bi
                bu_scr[pl.ds(r, BG), cr:cr + SCAN_W] = nr
                bu_scr[pl.ds(r, BG), ci:ci + SCAN_W] = ni
                return nr, ni

            xr0 = st_ref[:, cr:cr + SCAN_W]
            xi0 = st_ref[:, ci:ci + SCAN_W]
            xr, xi = lax.fori_loop(0, CHUNK, scan_body, (xr0, xi0), unroll=True)
            st_ref[:, cr:cr + SCAN_W] = jnp.where(live, xr, xr0)
            st_ref[:, ci:ci + SCAN_W] = jnp.where(live, xi, xi0)

    def s5_out(h):
        xs = bu_scr[pl.ds(pl.multiple_of(row0, ROWS), ROWS), h * HALF_STATE:(h + 1) * HALF_STATE]
        yh = jnp.dot(xs.astype(BF16), ccat_ref[h],
                     preferred_element_type=F32)
        for jj in range(2):
            ytb_scr[2 * h + jj] = yh[:, jj * LANES:(jj + 1) * LANES]

    def glu_in():
        for b in range(BG):
            for j in range(U_SLABS):
                ybt_scr[b * CHUNK:(b + 1) * CHUNK, j * LANES:(j + 1) * LANES] = (
                    ytb_scr[j, pl.ds(b, CHUNK, stride=BG), :])
        ybt_scr[...] = jax.nn.gelu(ybt_scr[...] + du_scr[...])

    def glu_out():
        zs = ybt_scr[...]
        gate = jnp.dot(zs.astype(BF16), wglu_ref[...], preferred_element_type=F32) + bglu_ref[...]
        s_out = zs * jax.nn.sigmoid(gate)
        merged_scr[:, ATTN_W:] = _rms(s_out, gs_ref[...]).astype(BF16)

    def attn_norm():
        merged_scr[:, :ATTN_W] = _rms(attn_scr[...], ga_ref[...]).astype(BF16)

    pieces = {"unpack": unpack, "norm": proj_norm, "W": proj, "QKN": proj_qknorm, "OP": out_proj,
              "S": attn_scores, "PV": attn_values, "BU": s5_in, "SCAN": s5_scan, "C": s5_out,
              "GLU_IN": glu_in, "GLU_OUT": glu_out, "AN": attn_norm}
    for item in _STEP_ORDER.split():
        name, _, arg = item.partition(":")
        pieces[name](*([int(arg)] if arg else []))


def _mlp_kernel(x_ref, d_ref, g2_ref, wup_ref, wdown_ref, o_ref):
    for r in range(0, MLP_ROWS, MLP_SUB_ROWS):
        rows = slice(r, r + MLP_SUB_ROWS)
        h = x_ref[rows, :] + d_ref[rows, :]
        xn = _rms(h, g2_ref[...]).astype(BF16)
        acc = h
        for j in range(D_FF // FF_BLK):
            f = jnp.dot(xn, wup_ref[:, j * FF_BLK:(j + 1) * FF_BLK], preferred_element_type=F32)
            f = jnp.maximum(f, 0.0)
            acc = acc + jnp.dot((f * f).astype(BF16), wdown_ref[j * FF_BLK:(j + 1) * FF_BLK, :],
                                preferred_element_type=F32)
        o_ref[rows, :] = acc


def _mix_chunk(s, n_steps):
    return jnp.clip(s - 1, 0, n_steps - 1)


def _const_spec(shape):
    n = len(shape)
    return pl.BlockSpec(shape, lambda *_: (0,) * n, pipeline_mode=pl.Buffered(1))


def _meta_call(meta, w):
    return pl.pallas_call(
        _meta_kernel,
        out_shape=(jax.ShapeDtypeStruct((N_META, KV_W), F32),
                   jax.ShapeDtypeStruct((N_META, KV_W), F32),
                   jax.ShapeDtypeStruct((1, STATE_W), F32)),
        scratch_shapes=[pltpu.VMEM((N_META, STATE_W), F32)],
        compiler_params=pltpu.CompilerParams(vmem_limit_bytes=VMEM_LIMIT_BYTES),
        name="meta",
    )(meta, w["g1"], w["w_in"], w["e"], w["gqk"], w["a"], w["bbd"])


def _mixer_call(x, st0, k0, v0, w, *, chunk_offset, n_out):
    nb, s_len, _ = x.shape
    n_chunks = s_len // CHUNK
    n_bg = nb // BG
    n_steps = n_bg * n_chunks
    assert n_chunks % n_out == 0
    grid = (n_steps + 2,)

    def chunk_block(gidx):
        return (gidx // n_chunks, gidx % n_chunks, 0)

    x_index = lambda s: chunk_block(jnp.minimum(s, n_steps - 1))
    d_index = lambda s: chunk_block(jnp.clip(s - 2, 0, n_steps - 1))
    per_bg = lambda s: (_mix_chunk(s, n_steps) // n_chunks, 0, 0)
    per_bg2 = lambda s: (_mix_chunk(s, n_steps) // n_chunks, 0)
    in_specs = [
        pl.BlockSpec((BG, CHUNK, D_MODEL), x_index),
        pl.BlockSpec((BG, STATE_W), per_bg2),
        pl.BlockSpec((BG, N_META + WINDOW, KV_W), per_bg),
        pl.BlockSpec((BG, N_META + WINDOW, KV_W), per_bg),
        _const_spec((1, D_MODEL)),
        _const_spec((D_MODEL, IN_W)),
        _const_spec((E_TILE, E_TILE)),
        _const_spec((1, QK_W)),
        _const_spec((N_Q_HEADS, KEY_ROWS)),
        pl.BlockSpec(memory_space=pltpu.SMEM),
        _const_spec((4, HALF_RI)),
        _const_spec((2, HALF_U, HALF_STATE)),
        _const_spec((2, HALF_STATE, HALF_U)),
        _const_spec((1, SSM_W)),
        _const_spec((SSM_W, SSM_W)),
        _const_spec((1, SSM_W)),
        _const_spec((1, ATTN_W)),
        _const_spec((1, SSM_W)),
        _const_spec((D_MODEL, D_MODEL)),
    ]
    out_specs = (
        pl.BlockSpec((BG, CHUNK, D_MODEL), d_index),
        pl.BlockSpec((BG, n_out * CHUNK, KV_W), per_bg),
        pl.BlockSpec((BG, n_out * CHUNK, KV_W), per_bg),
        pl.BlockSpec((BG, STATE_W), per_bg2),
    )
    out_shape = (
        jax.ShapeDtypeStruct((nb, s_len, D_MODEL), F32),
        jax.ShapeDtypeStruct((nb, n_out * CHUNK, KV_W), F32),
        jax.ShapeDtypeStruct((nb, n_out * CHUNK, KV_W), F32),
        jax.ShapeDtypeStruct((nb, STATE_W), F32),
    )
    scratch = [
        pltpu.VMEM((ROWS, IN_W), F32),
        pltpu.VMEM((ROWS, D_MODEL), BF16),
        pltpu.VMEM((BG, N_Q_HEADS * CHUNK, LANES), BF16),
        pltpu.VMEM((ROWS, SSM_W), F32),
        pltpu.VMEM((ROWS, ATTN_W), F32),
        pltpu.VMEM((U_SLABS, ROWS, LANES), F32),
        pltpu.VMEM((ROWS, STATE_W), F32),
        pltpu.VMEM((U_SLABS, ROWS, LANES), F32),
        pltpu.VMEM((ROWS, SSM_W), F32),
        pltpu.VMEM((BG, KEY_ROWS, KV_W), BF16),
        pltpu.VMEM((BG, KEY_ROWS, KV_W), BF16),
        pltpu.VMEM((ROWS, D_MODEL), BF16),
    ]
    kern = functools.partial(_mixer_kernel, n_chunks=n_chunks, n_bg=n_bg,
                             chunk_offset=chunk_offset, n_out=n_out)
    return pl.pallas_call(
        kern, grid=grid, in_specs=in_specs, out_specs=out_specs, out_shape=out_shape,
        scratch_shapes=scratch,
        compiler_params=pltpu.CompilerParams(
            dimension_semantics=("arbitrary",), vmem_limit_bytes=VMEM_LIMIT_BYTES),
        name="mixer",
    )(x, st0, k0, v0,
      w["g1"], w["w_in"], w["e"], w["gqk"], w["fill"], jnp.zeros((1,), jnp.int32),
      w["a"], w["bbd"], w["ccat"], w["dvec"],
      w["w_glu"], w["b_glu"], w["ga"], w["gs"], w["w_out"])


def _mlp_call(x, d, w):
    n = x.shape[0]
    return pl.pallas_call(
        _mlp_kernel,
        grid=(n // MLP_ROWS,),
        in_specs=[pl.BlockSpec((MLP_ROWS, D_MODEL), lambda i: (i, 0)),
                  pl.BlockSpec((MLP_ROWS, D_MODEL), lambda i: (i, 0)),
                  _const_spec((1, D_MODEL)),
                  _const_spec((D_MODEL, D_FF)),
                  _const_spec((D_FF, D_MODEL))],
        out_specs=pl.BlockSpec((MLP_ROWS, D_MODEL), lambda i: (i, 0)),
        out_shape=jax.ShapeDtypeStruct((n, D_MODEL), F32),
        compiler_params=pltpu.CompilerParams(
            dimension_semantics=("arbitrary",), vmem_limit_bytes=VMEM_LIMIT_BYTES),
        name="mlp",
    )(x, d, w["g2"], w["w_up"], w["w_down"])


def _prep_weights(norm1_g, w_in, q_norm_g, k_norm_g, sinks, ssm_A_re, ssm_A_im, ssm_log_dt,
                  ssm_B_re, ssm_B_im, ssm_C_re, ssm_C_im, ssm_D, w_glu, b_glu, attn_out_g,
                  ssm_out_g, w_out, norm2_g, w_up, w_down):
    def attn_perm(m):
        lead = m.shape[:-1]
        return m.reshape(lead + (N_KV_HEADS, REP, HEAD_DIM)).swapaxes(-3, -2).reshape(lead + (ATTN_W,))

    w_in_p = jnp.concatenate([attn_perm(w_in[:, :ATTN_W]), w_in[:, ATTN_W:]], axis=1).astype(BF16)
    e = jnp.kron(jnp.eye(E_TILE // HEAD_DIM, dtype=F32),
                 jnp.full((HEAD_DIM, HEAD_DIM), 1.0 / HEAD_DIM, F32)).astype(BF16)
    gqk = jnp.concatenate([jnp.tile(q_norm_g, N_Q_HEADS) * (HEAD_DIM ** -0.5 * LOG2E),
                           jnp.tile(k_norm_g, N_KV_HEADS)])[None, :]

    fill = jnp.full((N_Q_HEADS, KEY_ROWS), NEG_INF, F32).at[:, N_KEYS].set(sinks * LOG2E)

    lam_re = jnp.minimum(ssm_A_re, EIG_CLIP)
    lam_im = ssm_A_im
    dt = jnp.exp(ssm_log_dt)[:, None]
    mag = jnp.exp(lam_re * dt)
    ar = mag * jnp.cos(lam_im * dt)
    ai = mag * jnp.sin(lam_im * dt)
    den = lam_re * lam_re + lam_im * lam_im
    cr = ((ar - 1.0) * lam_re + ai * lam_im) / den
    ci = (ai * lam_re - (ar - 1.0) * lam_im) / den
    bb_re = cr[..., None] * ssm_B_re - ci[..., None] * ssm_B_im
    bb_im = cr[..., None] * ssm_B_im + ci[..., None] * ssm_B_re
    a = jnp.stack([ar.reshape(2, HALF_RI), ai.reshape(2, HALF_RI)], axis=1).reshape(4, HALF_RI)
    eye = jnp.eye(HALF_G, dtype=F32)
    bb = jnp.stack([bb_re, bb_im], 0).reshape(2, 2, HALF_G, SSM_P, SSM_CH)
    bbd = jnp.einsum('ab,rhapc->hacrbp', eye, bb).reshape(2, HALF_U, HALF_STATE).astype(BF16)
    cc = jnp.stack([ssm_C_re, -ssm_C_im], 0).reshape(2, 2, HALF_G, SSM_CH, SSM_P)
    ccat = jnp.einsum('ab,rhacp->hrapbc', eye, cc).reshape(2, HALF_STATE, HALF_U).astype(BF16)

    w_out_p = jnp.concatenate(
        [attn_perm(w_out[:ATTN_W].T).T, w_out[ATTN_W:]], axis=0).astype(BF16)
    return {
        "g1": norm1_g[None, :], "w_in": w_in_p, "e": e, "gqk": gqk, "fill": fill,
        "a": a, "bbd": bbd, "ccat": ccat, "dvec": ssm_D.reshape(1, SSM_W),
        "w_glu": w_glu.astype(BF16), "b_glu": b_glu[None, :],
        "ga": attn_perm(attn_out_g)[None, :], "gs": ssm_out_g[None, :], "w_out": w_out_p,
        "g2": norm2_g[None, :], "w_up": w_up.astype(BF16), "w_down": w_down.astype(BF16),
    }


def _state_to_lanes(re, im):
    nb = re.shape[0]
    return jnp.stack([re.reshape(nb, 2, HALF_RI), im.reshape(nb, 2, HALF_RI)], axis=2).reshape(nb, STATE_W)


def _lanes_to_state(st):
    nb = st.shape[0]
    s = st.reshape(nb, 2, 2, HALF_RI)
    return s[:, :, 0].reshape(nb, SSM_G, SSM_P), s[:, :, 1].reshape(nb, SSM_G, SSM_P)


def kernel(x_prompt, x_sample, cache_swa_k, cache_swa_v, state_ssm_re, state_ssm_im, meta_tokens, norm1_g, w_in, q_norm_g, k_norm_g, sinks, ssm_A_re, ssm_A_im, ssm_log_dt, ssm_B_re, ssm_B_im, ssm_C_re, ssm_C_im, ssm_D, w_glu, b_glu, attn_out_g, ssm_out_g, w_out, norm2_g, w_up, w_down):
    assert norm1_g.shape[0] == 1, "single-layer trunk"
    nb, seq, _ = x_prompt.shape
    nd, dseq, _ = x_sample.shape
    assert dseq == CHUNK and seq % CHUNK == 0 and seq >= WINDOW and nb % BG == 0 and nd % BG == 0
    w = _prep_weights(norm1_g[0], w_in[0], q_norm_g[0], k_norm_g[0], sinks[0], ssm_A_re[0],
                      ssm_A_im[0], ssm_log_dt[0], ssm_B_re[0], ssm_B_im[0], ssm_C_re[0],
                      ssm_C_im[0], ssm_D[0], w_glu[0], b_glu[0], attn_out_g[0], ssm_out_g[0],
                      w_out[0], norm2_g[0], w_up[0], w_down[0])

    k_meta, v_meta, st_meta = _meta_call(meta_tokens, w)
    kmeta_b = jnp.broadcast_to(k_meta[None], (nb, N_META, KV_W))
    vmeta_b = jnp.broadcast_to(v_meta[None], (nb, N_META, KV_W))
    empty_win = ((0, 0), (0, WINDOW), (0, 0))
    dp, kp, vp, stp = _mixer_call(
        x_prompt, jnp.broadcast_to(st_meta, (nb, STATE_W)), jnp.pad(kmeta_b, empty_win),
        jnp.pad(vmeta_b, empty_win), w, chunk_offset=0, n_out=WINDOW // CHUNK)
    y_prompt = _mlp_call(x_prompt.reshape(nb * seq, D_MODEL), dp.reshape(nb * seq, D_MODEL),
                         w).reshape(nb, seq, D_MODEL)

    ck = cache_swa_k[0].reshape(nd, N_META + WINDOW, KV_W)
    cv = cache_swa_v[0].reshape(nd, N_META + WINDOW, KV_W)
    ds, ks, vs, sts = _mixer_call(
        x_sample, _state_to_lanes(state_ssm_re[0], state_ssm_im[0]),
        ck, cv, w, chunk_offset=WINDOW // CHUNK, n_out=1)
    y_sample = _mlp_call(x_sample.reshape(nd * dseq, D_MODEL), ds.reshape(nd * dseq, D_MODEL),
                         w).reshape(nd, dseq, D_MODEL)

    kv5 = lambda t: t.reshape(t.shape[0], t.shape[1], N_KV_HEADS, HEAD_DIM)[None]
    new_k_prompt = kv5(jnp.concatenate([kmeta_b, kp], axis=1))
    new_v_prompt = kv5(jnp.concatenate([vmeta_b, vp], axis=1))
    srp, sip = _lanes_to_state(stp)
    srs, sis = _lanes_to_state(sts)
    return (y_prompt, y_sample, new_k_prompt, new_v_prompt, srp[None], sip[None],
            kv5(ks), kv5(vs), srs[None], sis[None])
```

```python
import functools
import math

import jax
import jax.numpy as jnp
from jax import lax
from jax.experimental import pallas as pl
from jax.experimental.pallas import tpu as pltpu

D_MODEL = 1024
CHUNK = 64
N_META = 16
HEAD_DIM = 64
ATTN_W = 512
N_KV_HEADS = 2
REP = 4
N_Q_HEADS = N_KV_HEADS * REP
KV_W = N_KV_HEADS * HEAD_DIM
WINDOW = 128
SSM_W = 512
SSM_CH = 16
SSM_G = 32
SSM_P = 64
D_FF = 4 * D_MODEL
IN_W = ATTN_W + 2 * KV_W + SSM_W
EPS = 1e-6
EIG_CLIP = -1e-4
NEG_INF = -1e30

SUBLANES = 8
LANES = 128
V7X_VMEM_BYTES = 64 * 1024 * 1024
COMPILER_VMEM_BYTES = 12 * 1024 * 1024

BG = SUBLANES
ROWS = BG * CHUNK
QK_W = ATTN_W + KV_W
V_OFF = QK_W
U_OFF = QK_W + KV_W
N_KEYS = N_META + WINDOW + CHUNK
KEY_ROWS = 256
LOG2E = 1.4426950408889634
N_SLOTS = WINDOW // CHUNK + 1
HALF_G = SSM_G // 2
HALF_RI = HALF_G * SSM_P
HALF_STATE = 2 * HALF_RI
STATE_W = 2 * HALF_STATE
HALF_U = HALF_G * SSM_CH
U_SLABS = SSM_W // LANES
SCAN_W = 512
MLP_ROWS = 1024
MLP_SUB_ROWS = 512
E_TILE = 256
MXU_N = 256
FF_BLK = 1024

BF16 = jnp.bfloat16
F32 = jnp.float32

_STEP_ORDER = (
    "unpack norm GLU_OUT AN BU:0 S:0 BU:1 SCAN:0 S:1 OP:0 PV:0 S:2 OP:1 SCAN:1 PV:1 S:3 OP:2 PV:2 S:4 OP:3 "
    "PV:3 S:5 W:0 PV:4 S:6 W:1 PV:5 S:7 W:2 PV:6 QKN C:0 PV:7 W:3 C:1 W:4 GLU_IN")


def _rms(x, g):
    ms = jnp.mean(x * x, axis=-1, keepdims=True)
    return x * lax.rsqrt(ms + EPS) * g


def _project(x, g1_ref, win_ref, e_ref, gqk_ref):
    xn = _rms(x, g1_ref[...]).astype(BF16)
    z = jnp.dot(xn, win_ref[...], preferred_element_type=F32)
    zqk = z[:, :QK_W]
    sq = (zqk * zqk).astype(BF16)
    ms = jnp.concatenate(
        [jnp.dot(sq[:, o:o + n], e_ref[:n, :n], preferred_element_type=F32)
         for o, n in ((0, E_TILE), (E_TILE, E_TILE), (2 * E_TILE, QK_W - 2 * E_TILE))], axis=1)
    qk = zqk * lax.rsqrt(ms + EPS) * gqk_ref[...]
    return qk, z


def _meta_kernel(meta_ref, g1_ref, win_ref, e_ref, gqk_ref, a_ref, bbd_ref,
                 k_ref, v_ref, st_ref, bu_scr):
    qk, z = _project(meta_ref[...], g1_ref, win_ref, e_ref, gqk_ref)
    k_ref[...] = qk[:, ATTN_W:]
    v_ref[...] = z[:, V_OFF:U_OFF]
    u = z[:, U_OFF:].astype(BF16)
    for h in range(2):
        bu_scr[:, h * HALF_STATE:(h + 1) * HALF_STATE] = jnp.dot(
            u[:, h * HALF_U:(h + 1) * HALF_U], bbd_ref[h], preferred_element_type=F32)
    for h in range(2):
        ar = a_ref[2 * h:2 * h + 1, :]
        ai = a_ref[2 * h + 1:2 * h + 2, :]
        cr = h * HALF_STATE
        ci = cr + HALF_RI
        xr = jnp.zeros((1, HALF_RI), F32)
        xi = jnp.zeros((1, HALF_RI), F32)
        for t in range(N_META):
            br = bu_scr[t:t + 1, cr:cr + HALF_RI]
            bi = bu_scr[t:t + 1, ci:ci + HALF_RI]
            xr, xi = ar * xr - ai * xi + br, ar * xi + ai * xr + bi
        st_ref[:, cr:cr + HALF_RI] = xr
        st_ref[:, ci:ci + HALF_RI] = xi


def _mixer_kernel(x_ref, st0_ref, k0_ref, v0_ref,
                  g1_ref, win_ref, e_ref, gqk_ref, fill_ref, row0_ref, a_ref, bbd_ref, ccat_ref, dvec_ref,
                  wglu_ref, bglu_ref, ga_ref, gs_ref, wout_ref,
                  d_ref, kout_ref, vout_ref, st_ref,
                  z_scr, xn_scr, qs_scr, du_scr, attn_scr, utb_scr, bu_scr, ytb_scr, ybt_scr, kwin, vwin,
                  merged_scr, *, n_chunks, n_bg, chunk_offset, n_out):
    s = pl.program_id(0)
    n_steps = n_bg * n_chunks
    c = _mix_chunk(s, n_steps) % n_chunks
    live = jnp.logical_and(s >= 1, s <= n_steps)
    gc = c + chunk_offset
    row0 = row0_ref[0]

    @pl.when(s == 0)
    def _():
        z_scr[...] = jnp.zeros((ROWS, IN_W), F32)
        ybt_scr[...] = jnp.zeros((ROWS, SSM_W), F32)
        attn_scr[...] = jnp.zeros((ROWS, ATTN_W), F32)

    @pl.when(jnp.logical_and(c == 0, s <= n_steps))
    def _():
        st_ref[...] = st0_ref[...]
        kwin[:, 0:N_META + WINDOW, :] = k0_ref[...].astype(BF16)
        vwin[:, 0:N_META + WINDOW, :] = v0_ref[...].astype(BF16)
        kwin[:, N_META + WINDOW:, :] = jnp.zeros((BG, KEY_ROWS - N_META - WINDOW, KV_W), BF16)
        vwin[:, N_META + WINDOW:, :] = jnp.zeros((BG, KEY_ROWS - N_META - WINDOW, KV_W), BF16)

    lane = lax.broadcasted_iota(jnp.int32, (CHUNK, LANES), 1)
    head_lanes = (lane < HEAD_DIM, lane >= HEAD_DIM)
    key_limit = jnp.where(gc >= N_SLOTS - 1, N_KEYS, N_META + CHUNK * (gc + 1))
    key_ok = lax.broadcasted_iota(jnp.int32, (1, KEY_ROWS), 1) < key_limit

    def unpack():
        k = z_scr[:, ATTN_W:QK_W].reshape(BG, CHUNK, KV_W)
        v = z_scr[:, V_OFF:U_OFF].reshape(BG, CHUNK, KV_W)
        u = z_scr[:, U_OFF:]
        slot_row = pl.multiple_of(N_META + lax.rem(gc, N_SLOTS) * CHUNK, 16)
        kwin[:, pl.ds(slot_row, CHUNK), :] = k.astype(BF16)
        vwin[:, pl.ds(slot_row, CHUNK), :] = v.astype(BF16)
        out_row = pl.multiple_of(lax.rem(c, n_out) * CHUNK, CHUNK)
        kout_ref[:, pl.ds(out_row, CHUNK), :] = k
        vout_ref[:, pl.ds(out_row, CHUNK), :] = v
        for b in range(BG):
            qb = z_scr[b * CHUNK:(b + 1) * CHUNK, 0:ATTN_W]
            parts = [jnp.where(head_lanes[kvh], qb[:, r * LANES:(r + 1) * LANES], 0.0)
                     for kvh in range(N_KV_HEADS) for r in range(REP)]
            qs_scr[b] = jnp.concatenate(parts, axis=0).astype(BF16)
        for b in range(BG):
            for j in range(U_SLABS):
                utb_scr[j, pl.ds(b, CHUNK, stride=BG), :] = (
                    u[b * CHUNK:(b + 1) * CHUNK, j * LANES:(j + 1) * LANES])
        du_scr[...] = dvec_ref[...] * u

    def out_proj(j):
        cols = slice(j * MXU_N, (j + 1) * MXU_N)
        d_ref[:, :, cols] = jnp.dot(merged_scr[...], wout_ref[:, cols],
                                    preferred_element_type=F32).reshape(BG, CHUNK, MXU_N)

    def proj_norm():
        xn_scr[...] = _rms(x_ref[...].reshape(ROWS, D_MODEL), g1_ref[...]).astype(BF16)

    def proj(j):
        cols = slice(j * MXU_N, (j + 1) * MXU_N)
        z_scr[:, cols] = jnp.dot(xn_scr[...], win_ref[:, cols], preferred_element_type=F32)

    def proj_qknorm():
        for o, n in ((0, E_TILE), (E_TILE, E_TILE), (2 * E_TILE, QK_W - 2 * E_TILE)):
            zq = z_scr[:, o:o + n]
            ms = jnp.dot((zq * zq).astype(BF16), e_ref[:n, :n], preferred_element_type=F32)
            z_scr[:, o:o + n] = zq * lax.rsqrt(ms + EPS) * gqk_ref[:, o:o + n]

    probs = {}

    def attn_scores(b):
        sc = lax.dot_general(qs_scr[b], kwin[b], (((1,), (1,)), ((), ())), preferred_element_type=F32)
        p_parts = []
        inv_l = []
        for i in range(N_Q_HEADS):
            sb = jnp.where(key_ok, sc[i * CHUNK:(i + 1) * CHUNK], fill_ref[i:i + 1, :])
            m = jnp.max(sb, axis=-1, keepdims=True)
            p = jnp.exp2(sb - m)
            l = jnp.sum(p, axis=-1, keepdims=True)
            p_parts.append(p.astype(BF16))
            inv_l.append(1.0 / l)
        probs[b] = (jnp.concatenate(p_parts, axis=0), inv_l)

    def attn_values(b):
        p_all, inv_l = probs.pop(b)
        o = jnp.dot(p_all, vwin[b], preferred_element_type=F32)
        outs = []
        for r in range(REP):
            o0 = o[r * CHUNK:(r + 1) * CHUNK] * inv_l[r]
            o1 = o[(REP + r) * CHUNK:(REP + r + 1) * CHUNK] * inv_l[REP + r]
            outs.append(jnp.where(head_lanes[0], o0, o1))
        attn_scr[b * CHUNK:(b + 1) * CHUNK, :] = jnp.concatenate(outs, axis=1)

    def s5_in(h):
        uh = jnp.concatenate([utb_scr[2 * h], utb_scr[2 * h + 1]], axis=1).astype(BF16)
        bu_scr[:, h * HALF_STATE:(h + 1) * HALF_STATE] = jnp.dot(
            uh, bbd_ref[h], preferred_element_type=F32)

    def s5_scan(h):
        for w in range(HALF_RI // SCAN_W):
            cr = h * HALF_STATE + w * SCAN_W
            ci = cr + HALF_RI
            ar = jnp.broadcast_to(a_ref[2 * h:2 * h + 1, w * SCAN_W:(w + 1) * SCAN_W], (BG, SCAN_W))
            ai = jnp.broadcast_to(a_ref[2 * h + 1:2 * h + 2, w * SCAN_W:(w + 1) * SCAN_W], (BG, SCAN_W))

            def scan_body(t, carry, cr=cr, ci=ci, ar=ar, ai=ai):
                xr, xi = carry
                r = pl.multiple_of(t * BG, BG)
                rl = pl.multiple_of(row0 + t * BG, BG)
                br = bu_scr[pl.ds(rl, BG), cr:cr + SCAN_W]
                bi = bu_scr[pl.ds(rl, BG), ci:ci + SCAN_W]
                nr = ar * xr - ai * xi + br
                ni = ar * xi + ai * xr + bi
                bu_scr[pl.ds(r, BG), cr:cr + SCAN_W] = nr
                bu_scr[pl.ds(r, BG), ci:ci + SCAN_W] = ni
                return nr, ni

            xr0 = st_ref[:, cr:cr + SCAN_W]
            xi0 = st_ref[:, ci:ci + SCAN_W]
            xr, xi = lax.fori_loop(0, CHUNK, scan_body, (xr0, xi0), unroll=True)
            st_ref[:, cr:cr + SCAN_W] = jnp.where(live, xr, xr0)
            st_ref[:, ci:ci + SCAN_W] = jnp.where(live, xi, xi0)

    def s5_out(h):
        xs = bu_scr[pl.ds(pl.multiple_of(row0, ROWS), ROWS), h * HALF_STATE:(h + 1) * HALF_STATE]
        yh = jnp.dot(xs.astype(BF16), ccat_ref[h],
                     preferred_element_type=F32)
        for jj in range(2):
            ytb_scr[2 * h + jj] = yh[:, jj * LANES:(jj + 1) * LANES]

    def glu_in():
        for b in range(BG):
            for j in range(U_SLABS):
                ybt_scr[b * CHUNK:(b + 1) * CHUNK, j * LANES:(j + 1) * LANES] = (
                    ytb_scr[j, pl.ds(b, CHUNK, stride=BG), :])
        ybt_scr[...] = jax.nn.gelu(ybt_scr[...] + du_scr[...])

    def glu_out():
        zs = ybt_scr[...]
        gate = jnp.dot(zs.astype(BF16), wglu_ref[...], preferred_element_type=F32) + bglu_ref[...]
        s_out = zs * jax.nn.sigmoid(gate)
        merged_scr[:, ATTN_W:] = _rms(s_out, gs_ref[...]).astype(BF16)

    def attn_norm():
        merged_scr[:, :ATTN_W] = _rms(attn_scr[...], ga_ref[...]).astype(BF16)

    pieces = {"unpack": unpack, "norm": proj_norm, "W": proj, "QKN": proj_qknorm, "OP": out_proj,
              "S": attn_scores, "PV": attn_values, "BU": s5_in, "SCAN": s5_scan, "C": s5_out,
              "GLU_IN": glu_in, "GLU_OUT": glu_out, "AN": attn_norm}
    for item in _STEP_ORDER.split():
        name, _, arg = item.partition(":")
        pieces[name](*([int(arg)] if arg else []))


def _mlp_kernel(x_ref, d_ref, g2_ref, wup_ref, wdown_ref, o_ref):
    for r in range(0, MLP_ROWS, MLP_SUB_ROWS):
        rows = slice(r, r + MLP_SUB_ROWS)
        h = x_ref[rows, :] + d_ref[rows, :]
        xn = _rms(h, g2_ref[...]).astype(BF16)
        acc = h
        for j in range(D_FF // FF_BLK):
            f = jnp.dot(xn, wup_ref[:, j * FF_BLK:(j + 1) * FF_BLK], preferred_element_type=F32)
            f = jnp.maximum(f, 0.0)
            acc = acc + jnp.dot((f * f).astype(BF16), wdown_ref[j * FF_BLK:(j + 1) * FF_BLK, :],
                                preferred_element_type=F32)
        o_ref[rows, :] = acc


def _mix_chunk(s, n_steps):
    return jnp.clip(s - 1, 0, n_steps - 1)


def _const_spec(shape):
    n = len(shape)
    return pl.BlockSpec(shape, lambda *_: (0,) * n, pipeline_mode=pl.Buffered(1))


def _vmem_limit(blocks, scratch=()):
    need = COMPILER_VMEM_BYTES
    need += sum(n_buf * math.prod(shape) * jnp.dtype(dtype).itemsize for shape, dtype, n_buf in blocks)
    need += sum(math.prod(m.shape) * jnp.dtype(m.dtype).itemsize for m in scratch)
    assert need <= V7X_VMEM_BYTES, need
    return need


def _spec_blocks(specs, arrays):
    return [(spec.block_shape, a.dtype, spec.pipeline_mode.buffer_count if spec.pipeline_mode else 2)
            for spec, a in zip(specs, arrays) if spec.block_shape is not None]


def _meta_call(meta, w):
    args = (meta, w["g1"], w["w_in"], w["e"], w["gqk"], w["a"], w["bbd"])
    out_shape = (jax.ShapeDtypeStruct((N_META, KV_W), F32),
                 jax.ShapeDtypeStruct((N_META, KV_W), F32),
                 jax.ShapeDtypeStruct((1, STATE_W), F32))
    scratch = [pltpu.VMEM((N_META, STATE_W), F32)]
    whole = [(a.shape, a.dtype, 1) for a in args + out_shape]
    return pl.pallas_call(
        _meta_kernel, out_shape=out_shape, scratch_shapes=scratch,
        compiler_params=pltpu.CompilerParams(vmem_limit_bytes=_vmem_limit(whole, scratch)),
        name="meta",
    )(*args)


def _mixer_call(x, st0, k0, v0, w, *, chunk_offset, n_out):
    nb, s_len, _ = x.shape
    n_chunks = s_len // CHUNK
    n_bg = nb // BG
    n_steps = n_bg * n_chunks
    assert n_chunks % n_out == 0
    grid = (n_steps + 2,)

    def chunk_block(gidx):
        return (gidx // n_chunks, gidx % n_chunks, 0)

    x_index = lambda s: chunk_block(jnp.minimum(s, n_steps - 1))
    d_index = lambda s: chunk_block(jnp.clip(s - 2, 0, n_steps - 1))
    per_bg = lambda s: (_mix_chunk(s, n_steps) // n_chunks, 0, 0)
    per_bg2 = lambda s: (_mix_chunk(s, n_steps) // n_chunks, 0)
    in_specs = [
        pl.BlockSpec((BG, CHUNK, D_MODEL), x_index),
        pl.BlockSpec((BG, STATE_W), per_bg2),
        pl.BlockSpec((BG, N_META + WINDOW, KV_W), per_bg),
        pl.BlockSpec((BG, N_META + WINDOW, KV_W), per_bg),
        _const_spec((1, D_MODEL)),
        _const_spec((D_MODEL, IN_W)),
        _const_spec((E_TILE, E_TILE)),
        _const_spec((1, QK_W)),
        _const_spec((N_Q_HEADS, KEY_ROWS)),
        pl.BlockSpec(memory_space=pltpu.SMEM),
        _const_spec((4, HALF_RI)),
        _const_spec((2, HALF_U, HALF_STATE)),
        _const_spec((2, HALF_STATE, HALF_U)),
        _const_spec((1, SSM_W)),
        _const_spec((SSM_W, SSM_W)),
        _const_spec((1, SSM_W)),
        _const_spec((1, ATTN_W)),
        _const_spec((1, SSM_W)),
        _const_spec((D_MODEL, D_MODEL)),
    ]
    out_specs = (
        pl.BlockSpec((BG, CHUNK, D_MODEL), d_index),
        pl.BlockSpec((BG, n_out * CHUNK, KV_W), per_bg),
        pl.BlockSpec((BG, n_out * CHUNK, KV_W), per_bg),
        pl.BlockSpec((BG, STATE_W), per_bg2),
    )
    out_shape = (
        jax.ShapeDtypeStruct((nb, s_len, D_MODEL), F32),
        jax.ShapeDtypeStruct((nb, n_out * CHUNK, KV_W), F32),
        jax.ShapeDtypeStruct((nb, n_out * CHUNK, KV_W), F32),
        jax.ShapeDtypeStruct((nb, STATE_W), F32),
    )
    scratch = [
        pltpu.VMEM((ROWS, IN_W), F32),
        pltpu.VMEM((ROWS, D_MODEL), BF16),
        pltpu.VMEM((BG, N_Q_HEADS * CHUNK, LANES), BF16),
        pltpu.VMEM((ROWS, SSM_W), F32),
        pltpu.VMEM((ROWS, ATTN_W), F32),
        pltpu.VMEM((U_SLABS, ROWS, LANES), F32),
        pltpu.VMEM((ROWS, STATE_W), F32),
        pltpu.VMEM((U_SLABS, ROWS, LANES), F32),
        pltpu.VMEM((ROWS, SSM_W), F32),
        pltpu.VMEM((BG, KEY_ROWS, KV_W), BF16),
        pltpu.VMEM((BG, KEY_ROWS, KV_W), BF16),
        pltpu.VMEM((ROWS, D_MODEL), BF16),
    ]
    kern = functools.partial(_mixer_kernel, n_chunks=n_chunks, n_bg=n_bg,
                             chunk_offset=chunk_offset, n_out=n_out)
    args = (x, st0, k0, v0,
            w["g1"], w["w_in"], w["e"], w["gqk"], w["fill"], jnp.zeros((1,), jnp.int32),
            w["a"], w["bbd"], w["ccat"], w["dvec"],
            w["w_glu"], w["b_glu"], w["ga"], w["gs"], w["w_out"])
    blocks = _spec_blocks(in_specs, args) + _spec_blocks(out_specs, out_shape)
    return pl.pallas_call(
        kern, grid=grid, in_specs=in_specs, out_specs=out_specs, out_shape=out_shape,
        scratch_shapes=scratch,
        compiler_params=pltpu.CompilerParams(
            dimension_semantics=("arbitrary",), vmem_limit_bytes=_vmem_limit(blocks, scratch)),
        name="mixer",
    )(*args)


def _mlp_call(x, d, w):
    n = x.shape[0]
    rows = pl.BlockSpec((MLP_ROWS, D_MODEL), lambda i: (i, 0))
    in_specs = [rows, rows, _const_spec((1, D_MODEL)), _const_spec((D_MODEL, D_FF)),
                _const_spec((D_FF, D_MODEL))]
    args = (x, d, w["g2"], w["w_up"], w["w_down"])
    out_shape = jax.ShapeDtypeStruct((n, D_MODEL), F32)
    blocks = _spec_blocks(in_specs, args) + _spec_blocks([rows], [out_shape])
    return pl.pallas_call(
        _mlp_kernel, grid=(n // MLP_ROWS,), in_specs=in_specs, out_specs=rows, out_shape=out_shape,
        compiler_params=pltpu.CompilerParams(
            dimension_semantics=("arbitrary",), vmem_limit_bytes=_vmem_limit(blocks)),
        name="mlp",
    )(*args)


def _prep_weights(norm1_g, w_in, q_norm_g, k_norm_g, sinks, ssm_A_re, ssm_A_im, ssm_log_dt,
                  ssm_B_re, ssm_B_im, ssm_C_re, ssm_C_im, ssm_D, w_glu, b_glu, attn_out_g,
                  ssm_out_g, w_out, norm2_g, w_up, w_down):
    def attn_perm(m):
        lead = m.shape[:-1]
        return m.reshape(lead + (N_KV_HEADS, REP, HEAD_DIM)).swapaxes(-3, -2).reshape(lead + (ATTN_W,))

    w_in_p = jnp.concatenate([attn_perm(w_in[:, :ATTN_W]), w_in[:, ATTN_W:]], axis=1).astype(BF16)
    e = jnp.kron(jnp.eye(E_TILE // HEAD_DIM, dtype=F32),
                 jnp.full((HEAD_DIM, HEAD_DIM), 1.0 / HEAD_DIM, F32)).astype(BF16)
    gqk = jnp.concatenate([jnp.tile(q_norm_g, N_Q_HEADS) * (HEAD_DIM ** -0.5 * LOG2E),
                           jnp.tile(k_norm_g, N_KV_HEADS)])[None, :]

    fill = jnp.full((N_Q_HEADS, KEY_ROWS), NEG_INF, F32).at[:, N_KEYS].set(sinks * LOG2E)

    lam_re = jnp.minimum(ssm_A_re, EIG_CLIP)
    lam_im = ssm_A_im
    dt = jnp.exp(ssm_log_dt)[:, None]
    mag = jnp.exp(lam_re * dt)
    ar = mag * jnp.cos(lam_im * dt)
    ai = mag * jnp.sin(lam_im * dt)
    den = lam_re * lam_re + lam_im * lam_im
    cr = ((ar - 1.0) * lam_re + ai * lam_im) / den
    ci = (ai * lam_re - (ar - 1.0) * lam_im) / den
    bb_re = cr[..., None] * ssm_B_re - ci[..., None] * ssm_B_im
    bb_im = cr[..., None] * ssm_B_im + ci[..., None] * ssm_B_re
    a = jnp.stack([ar.reshape(2, HALF_RI), ai.reshape(2, HALF_RI)], axis=1).reshape(4, HALF_RI)
    eye = jnp.eye(HALF_G, dtype=F32)
    bb = jnp.stack([bb_re, bb_im], 0).reshape(2, 2, HALF_G, SSM_P, SSM_CH)
    bbd = jnp.einsum('ab,rhapc->hacrbp', eye, bb).reshape(2, HALF_U, HALF_STATE).astype(BF16)
    cc = jnp.stack([ssm_C_re, -ssm_C_im], 0).reshape(2, 2, HALF_G, SSM_CH, SSM_P)
    ccat = jnp.einsum('ab,rhacp->hrapbc', eye, cc).reshape(2, HALF_STATE, HALF_U).astype(BF16)

    w_out_p = jnp.concatenate(
        [attn_perm(w_out[:ATTN_W].T).T, w_out[ATTN_W:]], axis=0).astype(BF16)
    return {
        "g1": norm1_g[None, :], "w_in": w_in_p, "e": e, "gqk": gqk, "fill": fill,
        "a": a, "bbd": bbd, "ccat": ccat, "dvec": ssm_D.reshape(1, SSM_W),
        "w_glu": w_glu.astype(BF16), "b_glu": b_glu[None, :],
        "ga": attn_perm(attn_out_g)[None, :], "gs": ssm_out_g[None, :], "w_out": w_out_p,
        "g2": norm2_g[None, :], "w_up": w_up.astype(BF16), "w_down": w_down.astype(BF16),
    }


def _state_to_lanes(re, im):
    nb = re.shape[0]
    return jnp.stack([re.reshape(nb, 2, HALF_RI), im.reshape(nb, 2, HALF_RI)], axis=2).reshape(nb, STATE_W)


def _lanes_to_state(st):
    nb = st.shape[0]
    s = st.reshape(nb, 2, 2, HALF_RI)
    return s[:, :, 0].reshape(nb, SSM_G, SSM_P), s[:, :, 1].reshape(nb, SSM_G, SSM_P)


def kernel(x_prompt, x_sample, cache_swa_k, cache_swa_v, state_ssm_re, state_ssm_im, meta_tokens, norm1_g, w_in, q_norm_g, k_norm_g, sinks, ssm_A_re, ssm_A_im, ssm_log_dt, ssm_B_re, ssm_B_im, ssm_C_re, ssm_C_im, ssm_D, w_glu, b_glu, attn_out_g, ssm_out_g, w_out, norm2_g, w_up, w_down):
    assert norm1_g.shape[0] == 1, "single-layer trunk"
    nb, seq, _ = x_prompt.shape
    nd, dseq, _ = x_sample.shape
    assert dseq == CHUNK and seq % CHUNK == 0 and seq >= WINDOW and nb % BG == 0 and nd % BG == 0
    w = _prep_weights(norm1_g[0], w_in[0], q_norm_g[0], k_norm_g[0], sinks[0], ssm_A_re[0],
                      ssm_A_im[0], ssm_log_dt[0], ssm_B_re[0], ssm_B_im[0], ssm_C_re[0],
                      ssm_C_im[0], ssm_D[0], w_glu[0], b_glu[0], attn_out_g[0], ssm_out_g[0],
                      w_out[0], norm2_g[0], w_up[0], w_down[0])

    k_meta, v_meta, st_meta = _meta_call(meta_tokens, w)
    kmeta_b = jnp.broadcast_to(k_meta[None], (nb, N_META, KV_W))
    vmeta_b = jnp.broadcast_to(v_meta[None], (nb, N_META, KV_W))
    empty_win = ((0, 0), (0, WINDOW), (0, 0))
    dp, kp, vp, stp = _mixer_call(
        x_prompt, jnp.broadcast_to(st_meta, (nb, STATE_W)), jnp.pad(kmeta_b, empty_win),
        jnp.pad(vmeta_b, empty_win), w, chunk_offset=0, n_out=WINDOW // CHUNK)
    y_prompt = _mlp_call(x_prompt.reshape(nb * seq, D_MODEL), dp.reshape(nb * seq, D_MODEL),
                         w).reshape(nb, seq, D_MODEL)

    ck = cache_swa_k[0].reshape(nd, N_META + WINDOW, KV_W)
    cv = cache_swa_v[0].reshape(nd, N_META + WINDOW, KV_W)
    ds, ks, vs, sts = _mixer_call(
        x_sample, _state_to_lanes(state_ssm_re[0], state_ssm_im[0]),
        ck, cv, w, chunk_offset=WINDOW // CHUNK, n_out=1)
    y_sample = _mlp_call(x_sample.reshape(nd * dseq, D_MODEL), ds.reshape(nd * dseq, D_MODEL),
                         w).reshape(nd, dseq, D_MODEL)

    kv5 = lambda t: t.reshape(t.shape[0], t.shape[1], N_KV_HEADS, HEAD_DIM)[None]
    new_k_prompt = kv5(jnp.concatenate([kmeta_b, kp], axis=1))
    new_v_prompt = kv5(jnp.concatenate([vmeta_b, vp], axis=1))
    srp, sip = _lanes_to_state(stp)
    srs, sis = _lanes_to_state(sts)
    return (y_prompt, y_sample, new_k_prompt, new_v_prompt, srp[None], sip[None],
            kv5(ks), kv5(vs), srs[None], sis[None])
```

```python
import functools
import math

import jax
import jax.numpy as jnp
from jax import lax
from jax.experimental import pallas as pl
from jax.experimental.pallas import tpu as pltpu

D_MODEL = 1024
CHUNK = 64
N_META = 16
HEAD_DIM = 64
ATTN_W = 512
N_KV_HEADS = 2
REP = 4
N_Q_HEADS = N_KV_HEADS * REP
KV_W = N_KV_HEADS * HEAD_DIM
WINDOW = 128
SSM_W = 512
SSM_CH = 16
SSM_G = 32
SSM_P = 64
D_FF = 4 * D_MODEL
IN_W = ATTN_W + 2 * KV_W + SSM_W
EPS = 1e-6
EIG_CLIP = -1e-4
NEG_INF = -1e30

SUBLANES = 8
LANES = 128
V7X_VMEM_BYTES = 64 * 1024 * 1024
COMPILER_VMEM_BYTES = 12 * 1024 * 1024

BG = SUBLANES
ROWS = BG * CHUNK
QK_W = ATTN_W + KV_W
V_OFF = QK_W
U_OFF = QK_W + KV_W
N_KEYS = N_META + WINDOW + CHUNK
KEY_ROWS = 256
LOG2E = 1.4426950408889634
N_SLOTS = WINDOW // CHUNK + 1
HALF_G = SSM_G // 2
HALF_RI = HALF_G * SSM_P
HALF_STATE = 2 * HALF_RI
STATE_W = 2 * HALF_STATE
HALF_U = HALF_G * SSM_CH
U_SLABS = SSM_W // LANES
SCAN_W = 512
MLP_ROWS = 1024
MLP_SUB_ROWS = 512
E_TILE = 256
MXU_N = 256
FF_BLK = 1024

BF16 = jnp.bfloat16
F32 = jnp.float32

_STEP_ORDER = (
    "unpack norm GLU_OUT AN BU:0 S:0 BU:1 SCAN:0 S:1 OP:0 PV:0 S:2 OP:1 SCAN:1 PV:1 S:3 OP:2 PV:2 S:4 OP:3 "
    "PV:3 S:5 W:0 PV:4 S:6 W:1 PV:5 S:7 W:2 PV:6 QKN C:0 PV:7 W:3 C:1 W:4 GLU_IN")
_FILL_ORDER = "norm W:0 W:1 W:2 QKN W:3 W:4"
_DRAIN_ORDER = "GLU_OUT AN OP:0 OP:1 OP:2 OP:3"


def _rms(x, g):
    ms = jnp.mean(x * x, axis=-1, keepdims=True)
    return x * lax.rsqrt(ms + EPS) * g


def _project(x, g1_ref, win_ref, e_ref, gqk_ref):
    xn = _rms(x, g1_ref[...]).astype(BF16)
    z = jnp.dot(xn, win_ref[...], preferred_element_type=F32)
    zqk = z[:, :QK_W]
    sq = (zqk * zqk).astype(BF16)
    ms = jnp.concatenate(
        [jnp.dot(sq[:, o:o + n], e_ref[:n, :n], preferred_element_type=F32)
         for o, n in ((0, E_TILE), (E_TILE, E_TILE), (2 * E_TILE, QK_W - 2 * E_TILE))], axis=1)
    qk = zqk * lax.rsqrt(ms + EPS) * gqk_ref[...]
    return qk, z


def _meta_kernel(meta_ref, g1_ref, win_ref, e_ref, gqk_ref, a_ref, bbd_ref,
                 k_ref, v_ref, st_ref, bu_scr):
    qk, z = _project(meta_ref[...], g1_ref, win_ref, e_ref, gqk_ref)
    k_ref[...] = qk[:, ATTN_W:]
    v_ref[...] = z[:, V_OFF:U_OFF]
    u = z[:, U_OFF:].astype(BF16)
    for h in range(2):
        bu_scr[:, h * HALF_STATE:(h + 1) * HALF_STATE] = jnp.dot(
            u[:, h * HALF_U:(h + 1) * HALF_U], bbd_ref[h], preferred_element_type=F32)
    for h in range(2):
        ar = a_ref[2 * h:2 * h + 1, :]
        ai = a_ref[2 * h + 1:2 * h + 2, :]
        cr = h * HALF_STATE
        ci = cr + HALF_RI
        xr = jnp.zeros((1, HALF_RI), F32)
        xi = jnp.zeros((1, HALF_RI), F32)
        for t in range(N_META):
            br = bu_scr[t:t + 1, cr:cr + HALF_RI]
            bi = bu_scr[t:t + 1, ci:ci + HALF_RI]
            xr, xi = ar * xr - ai * xi + br, ar * xi + ai * xr + bi
        st_ref[:, cr:cr + HALF_RI] = xr
        st_ref[:, ci:ci + HALF_RI] = xi


def _mixer_kernel(x_ref, st0_ref, k0_ref, v0_ref,
                  g1_ref, win_ref, e_ref, gqk_ref, fill_ref, row0_ref, a_ref, bbd_ref, ccat_ref, dvec_ref,
                  wglu_ref, bglu_ref, ga_ref, gs_ref, wout_ref,
                  d_ref, kout_ref, vout_ref, st_ref,
                  z_scr, xn_scr, qs_scr, du_scr, attn_scr, utb_scr, bu_scr, ytb_scr, ybt_scr, kwin, vwin,
                  merged_scr, *, n_chunks, n_bg, chunk_offset, n_out):
    s = pl.program_id(0)
    n_steps = n_bg * n_chunks
    c = _mix_chunk(s, n_steps) % n_chunks
    gc = c + chunk_offset
    row0 = row0_ref[0]

    def start_group():
        st_ref[...] = st0_ref[...]
        kwin[:, 0:N_META + WINDOW, :] = k0_ref[...].astype(BF16)
        vwin[:, 0:N_META + WINDOW, :] = v0_ref[...].astype(BF16)
        kwin[:, N_META + WINDOW:, :] = jnp.zeros((BG, KEY_ROWS - N_META - WINDOW, KV_W), BF16)
        vwin[:, N_META + WINDOW:, :] = jnp.zeros((BG, KEY_ROWS - N_META - WINDOW, KV_W), BF16)

    lane = lax.broadcasted_iota(jnp.int32, (CHUNK, LANES), 1)
    head_lanes = (lane < HEAD_DIM, lane >= HEAD_DIM)
    key_limit = jnp.where(gc >= N_SLOTS - 1, N_KEYS, N_META + CHUNK * (gc + 1))
    key_ok = lax.broadcasted_iota(jnp.int32, (1, KEY_ROWS), 1) < key_limit

    def unpack():
        k = z_scr[:, ATTN_W:QK_W].reshape(BG, CHUNK, KV_W)
        v = z_scr[:, V_OFF:U_OFF].reshape(BG, CHUNK, KV_W)
        u = z_scr[:, U_OFF:]
        slot_row = pl.multiple_of(N_META + lax.rem(gc, N_SLOTS) * CHUNK, 16)
        kwin[:, pl.ds(slot_row, CHUNK), :] = k.astype(BF16)
        vwin[:, pl.ds(slot_row, CHUNK), :] = v.astype(BF16)
        out_row = pl.multiple_of(lax.rem(c, n_out) * CHUNK, CHUNK)
        kout_ref[:, pl.ds(out_row, CHUNK), :] = k
        vout_ref[:, pl.ds(out_row, CHUNK), :] = v
        for b in range(BG):
            qb = z_scr[b * CHUNK:(b + 1) * CHUNK, 0:ATTN_W]
            parts = [jnp.where(head_lanes[kvh], qb[:, r * LANES:(r + 1) * LANES], 0.0)
                     for kvh in range(N_KV_HEADS) for r in range(REP)]
            qs_scr[b] = jnp.concatenate(parts, axis=0).astype(BF16)
        for b in range(BG):
            for j in range(U_SLABS):
                utb_scr[j, pl.ds(b, CHUNK, stride=BG), :] = (
                    u[b * CHUNK:(b + 1) * CHUNK, j * LANES:(j + 1) * LANES])
        du_scr[...] = dvec_ref[...] * u

    def out_proj(j):
        cols = slice(j * MXU_N, (j + 1) * MXU_N)
        d_ref[:, :, cols] = jnp.dot(merged_scr[...], wout_ref[:, cols],
                                    preferred_element_type=F32).reshape(BG, CHUNK, MXU_N)

    def proj_norm():
        xn_scr[...] = _rms(x_ref[...].reshape(ROWS, D_MODEL), g1_ref[...]).astype(BF16)

    def proj(j):
        cols = slice(j * MXU_N, (j + 1) * MXU_N)
        z_scr[:, cols] = jnp.dot(xn_scr[...], win_ref[:, cols], preferred_element_type=F32)

    def proj_qknorm():
        for o, n in ((0, E_TILE), (E_TILE, E_TILE), (2 * E_TILE, QK_W - 2 * E_TILE)):
            zq = z_scr[:, o:o + n]
            ms = jnp.dot((zq * zq).astype(BF16), e_ref[:n, :n], preferred_element_type=F32)
            z_scr[:, o:o + n] = zq * lax.rsqrt(ms + EPS) * gqk_ref[:, o:o + n]

    probs = {}

    def attn_scores(b):
        sc = lax.dot_general(qs_scr[b], kwin[b], (((1,), (1,)), ((), ())), preferred_element_type=F32)
        p_parts = []
        inv_l = []
        for i in range(N_Q_HEADS):
            sb = jnp.where(key_ok, sc[i * CHUNK:(i + 1) * CHUNK], fill_ref[i:i + 1, :])
            m = jnp.max(sb, axis=-1, keepdims=True)
            p = jnp.exp2(sb - m)
            l = jnp.sum(p, axis=-1, keepdims=True)
            p_parts.append(p.astype(BF16))
            inv_l.append(1.0 / l)
        probs[b] = (jnp.concatenate(p_parts, axis=0), inv_l)

    def attn_values(b):
        p_all, inv_l = probs.pop(b)
        o = jnp.dot(p_all, vwin[b], preferred_element_type=F32)
        outs = []
        for r in range(REP):
            o0 = o[r * CHUNK:(r + 1) * CHUNK] * inv_l[r]
            o1 = o[(REP + r) * CHUNK:(REP + r + 1) * CHUNK] * inv_l[REP + r]
            outs.append(jnp.where(head_lanes[0], o0, o1))
        attn_scr[b * CHUNK:(b + 1) * CHUNK, :] = jnp.concatenate(outs, axis=1)

    def s5_in(h):
        uh = jnp.concatenate([utb_scr[2 * h], utb_scr[2 * h + 1]], axis=1).astype(BF16)
        bu_scr[:, h * HALF_STATE:(h + 1) * HALF_STATE] = jnp.dot(
            uh, bbd_ref[h], preferred_element_type=F32)

    def s5_scan(h):
        for w in range(HALF_RI // SCAN_W):
            cr = h * HALF_STATE + w * SCAN_W
            ci = cr + HALF_RI
            ar = jnp.broadcast_to(a_ref[2 * h:2 * h + 1, w * SCAN_W:(w + 1) * SCAN_W], (BG, SCAN_W))
            ai = jnp.broadcast_to(a_ref[2 * h + 1:2 * h + 2, w * SCAN_W:(w + 1) * SCAN_W], (BG, SCAN_W))

            def scan_body(t, carry, cr=cr, ci=ci, ar=ar, ai=ai):
                xr, xi = carry
                r = pl.multiple_of(t * BG, BG)
                rl = pl.multiple_of(row0 + t * BG, BG)
                br = bu_scr[pl.ds(rl, BG), cr:cr + SCAN_W]
                bi = bu_scr[pl.ds(rl, BG), ci:ci + SCAN_W]
                nr = ar * xr - ai * xi + br
                ni = ar * xi + ai * xr + bi
                bu_scr[pl.ds(r, BG), cr:cr + SCAN_W] = nr
                bu_scr[pl.ds(r, BG), ci:ci + SCAN_W] = ni
                return nr, ni

            xr0 = st_ref[:, cr:cr + SCAN_W]
            xi0 = st_ref[:, ci:ci + SCAN_W]
            xr, xi = lax.fori_loop(0, CHUNK, scan_body, (xr0, xi0), unroll=True)
            st_ref[:, cr:cr + SCAN_W] = xr
            st_ref[:, ci:ci + SCAN_W] = xi

    def s5_out(h):
        xs = bu_scr[pl.ds(pl.multiple_of(row0, ROWS), ROWS), h * HALF_STATE:(h + 1) * HALF_STATE]
        yh = jnp.dot(xs.astype(BF16), ccat_ref[h],
                     preferred_element_type=F32)
        for jj in range(2):
            ytb_scr[2 * h + jj] = yh[:, jj * LANES:(jj + 1) * LANES]

    def glu_in():
        for b in range(BG):
            for j in range(U_SLABS):
                ybt_scr[b * CHUNK:(b + 1) * CHUNK, j * LANES:(j + 1) * LANES] = (
                    ytb_scr[j, pl.ds(b, CHUNK, stride=BG), :])
        ybt_scr[...] = jax.nn.gelu(ybt_scr[...] + du_scr[...])

    def glu_out():
        zs = ybt_scr[...]
        gate = jnp.dot(zs.astype(BF16), wglu_ref[...], preferred_element_type=F32) + bglu_ref[...]
        s_out = zs * jax.nn.sigmoid(gate)
        merged_scr[:, ATTN_W:] = _rms(s_out, gs_ref[...]).astype(BF16)

    def attn_norm():
        merged_scr[:, :ATTN_W] = _rms(attn_scr[...], ga_ref[...]).astype(BF16)

    pieces = {"unpack": unpack, "norm": proj_norm, "W": proj, "QKN": proj_qknorm, "OP": out_proj,
              "S": attn_scores, "PV": attn_values, "BU": s5_in, "SCAN": s5_scan, "C": s5_out,
              "GLU_IN": glu_in, "GLU_OUT": glu_out, "AN": attn_norm}

    def run(order):
        for item in order.split():
            name, _, arg = item.partition(":")
            pieces[name](*([int(arg)] if arg else []))

    @pl.when(s == 0)
    def _():
        ybt_scr[...] = jnp.zeros((ROWS, SSM_W), F32)
        attn_scr[...] = jnp.zeros((ROWS, ATTN_W), F32)
        run(_FILL_ORDER)

    @pl.when(jnp.logical_and(s >= 1, s <= n_steps))
    def _():
        pl.when(c == 0)(start_group)
        run(_STEP_ORDER)

    @pl.when(s == n_steps + 1)
    def _():
        run(_DRAIN_ORDER)


def _mlp_kernel(x_ref, d_ref, g2_ref, wup_ref, wdown_ref, o_ref):
    for r in range(0, MLP_ROWS, MLP_SUB_ROWS):
        rows = slice(r, r + MLP_SUB_ROWS)
        h = x_ref[rows, :] + d_ref[rows, :]
        xn = _rms(h, g2_ref[...]).astype(BF16)
        acc = h
        for j in range(D_FF // FF_BLK):
            f = jnp.dot(xn, wup_ref[:, j * FF_BLK:(j + 1) * FF_BLK], preferred_element_type=F32)
            f = jnp.maximum(f, 0.0)
            acc = acc + jnp.dot((f * f).astype(BF16), wdown_ref[j * FF_BLK:(j + 1) * FF_BLK, :],
                                preferred_element_type=F32)
        o_ref[rows, :] = acc


def _mix_chunk(s, n_steps):
    return jnp.clip(s - 1, 0, n_steps - 1)


def _const_spec(shape):
    n = len(shape)
    return pl.BlockSpec(shape, lambda *_: (0,) * n, pipeline_mode=pl.Buffered(1))


def _vmem_limit(blocks, scratch=()):
    need = COMPILER_VMEM_BYTES
    need += sum(n_buf * math.prod(shape) * jnp.dtype(dtype).itemsize for shape, dtype, n_buf in blocks)
    need += sum(math.prod(m.shape) * jnp.dtype(m.dtype).itemsize for m in scratch)
    assert need <= V7X_VMEM_BYTES, need
    return need


def _spec_blocks(specs, arrays):
    return [(spec.block_shape, a.dtype, spec.pipeline_mode.buffer_count if spec.pipeline_mode else 2)
            for spec, a in zip(specs, arrays) if spec.block_shape is not None]


def _meta_call(meta, w):
    args = (meta, w["g1"], w["w_in"], w["e"], w["gqk"], w["a"], w["bbd"])
    out_shape = (jax.ShapeDtypeStruct((N_META, KV_W), F32),
                 jax.ShapeDtypeStruct((N_META, KV_W), F32),
                 jax.ShapeDtypeStruct((1, STATE_W), F32))
    scratch = [pltpu.VMEM((N_META, STATE_W), F32)]
    whole = [(a.shape, a.dtype, 1) for a in args + out_shape]
    return pl.pallas_call(
        _meta_kernel, out_shape=out_shape, scratch_shapes=scratch,
        compiler_params=pltpu.CompilerParams(vmem_limit_bytes=_vmem_limit(whole, scratch)),
        name="meta",
    )(*args)


def _mixer_call(x, st0, k0, v0, w, *, chunk_offset, n_out):
    nb, s_len, _ = x.shape
    n_chunks = s_len // CHUNK
    n_bg = nb // BG
    n_steps = n_bg * n_chunks
    assert n_chunks % n_out == 0
    grid = (n_steps + 2,)

    def chunk_block(gidx):
        return (gidx // n_chunks, gidx % n_chunks, 0)

    x_index = lambda s: chunk_block(jnp.minimum(s, n_steps - 1))
    d_index = lambda s: chunk_block(jnp.clip(s - 2, 0, n_steps - 1))
    per_bg = lambda s: (_mix_chunk(s, n_steps) // n_chunks, 0, 0)
    per_bg2 = lambda s: (_mix_chunk(s, n_steps) // n_chunks, 0)
    in_specs = [
        pl.BlockSpec((BG, CHUNK, D_MODEL), x_index),
        pl.BlockSpec((BG, STATE_W), per_bg2),
        pl.BlockSpec((BG, N_META + WINDOW, KV_W), per_bg),
        pl.BlockSpec((BG, N_META + WINDOW, KV_W), per_bg),
        _const_spec((1, D_MODEL)),
        _const_spec((D_MODEL, IN_W)),
        _const_spec((E_TILE, E_TILE)),
        _const_spec((1, QK_W)),
        _const_spec((N_Q_HEADS, KEY_ROWS)),
        pl.BlockSpec(memory_space=pltpu.SMEM),
        _const_spec((4, HALF_RI)),
        _const_spec((2, HALF_U, HALF_STATE)),
        _const_spec((2, HALF_STATE, HALF_U)),
        _const_spec((1, SSM_W)),
        _const_spec((SSM_W, SSM_W)),
        _const_spec((1, SSM_W)),
        _const_spec((1, ATTN_W)),
        _const_spec((1, SSM_W)),
        _const_spec((D_MODEL, D_MODEL)),
    ]
    out_specs = (
        pl.BlockSpec((BG, CHUNK, D_MODEL), d_index),
        pl.BlockSpec((BG, n_out * CHUNK, KV_W), per_bg),
        pl.BlockSpec((BG, n_out * CHUNK, KV_W), per_bg),
        pl.BlockSpec((BG, STATE_W), per_bg2),
    )
    out_shape = (
        jax.ShapeDtypeStruct((nb, s_len, D_MODEL), F32),
        jax.ShapeDtypeStruct((nb, n_out * CHUNK, KV_W), F32),
        jax.ShapeDtypeStruct((nb, n_out * CHUNK, KV_W), F32),
        jax.ShapeDtypeStruct((nb, STATE_W), F32),
    )
    scratch = [
        pltpu.VMEM((ROWS, IN_W), F32),
        pltpu.VMEM((ROWS, D_MODEL), BF16),
        pltpu.VMEM((BG, N_Q_HEADS * CHUNK, LANES), BF16),
        pltpu.VMEM((ROWS, SSM_W), F32),
        pltpu.VMEM((ROWS, ATTN_W), F32),
        pltpu.VMEM((U_SLABS, ROWS, LANES), F32),
        pltpu.VMEM((ROWS, STATE_W), F32),
        pltpu.VMEM((U_SLABS, ROWS, LANES), F32),
        pltpu.VMEM((ROWS, SSM_W), F32),
        pltpu.VMEM((BG, KEY_ROWS, KV_W), BF16),
        pltpu.VMEM((BG, KEY_ROWS, KV_W), BF16),
        pltpu.VMEM((ROWS, D_MODEL), BF16),
    ]
    kern = functools.partial(_mixer_kernel, n_chunks=n_chunks, n_bg=n_bg,
                             chunk_offset=chunk_offset, n_out=n_out)
    args = (x, st0, k0, v0,
            w["g1"], w["w_in"], w["e"], w["gqk"], w["fill"], jnp.zeros((1,), jnp.int32),
            w["a"], w["bbd"], w["ccat"], w["dvec"],
            w["w_glu"], w["b_glu"], w["ga"], w["gs"], w["w_out"])
    blocks = _spec_blocks(in_specs, args) + _spec_blocks(out_specs, out_shape)
    return pl.pallas_call(
        kern, grid=grid, in_specs=in_specs, out_specs=out_specs, out_shape=out_shape,
        scratch_shapes=scratch,
        compiler_params=pltpu.CompilerParams(
            dimension_semantics=("arbitrary",), vmem_limit_bytes=_vmem_limit(blocks, scratch)),
        name="mixer",
    )(*args)


def _mlp_call(x, d, w):
    n = x.shape[0]
    rows = pl.BlockSpec((MLP_ROWS, D_MODEL), lambda i: (i, 0))
    in_specs = [rows, rows, _const_spec((1, D_MODEL)), _const_spec((D_MODEL, D_FF)),
                _const_spec((D_FF, D_MODEL))]
    args = (x, d, w["g2"], w["w_up"], w["w_down"])
    out_shape = jax.ShapeDtypeStruct((n, D_MODEL), F32)
    blocks = _spec_blocks(in_specs, args) + _spec_blocks([rows], [out_shape])
    return pl.pallas_call(
        _mlp_kernel, grid=(n // MLP_ROWS,), in_specs=in_specs, out_specs=rows, out_shape=out_shape,
        compiler_params=pltpu.CompilerParams(
            dimension_semantics=("arbitrary",), vmem_limit_bytes=_vmem_limit(blocks)),
        name="mlp",
    )(*args)


def _prep_weights(norm1_g, w_in, q_norm_g, k_norm_g, sinks, ssm_A_re, ssm_A_im, ssm_log_dt,
                  ssm_B_re, ssm_B_im, ssm_C_re, ssm_C_im, ssm_D, w_glu, b_glu, attn_out_g,
                  ssm_out_g, w_out, norm2_g, w_up, w_down):
    def attn_perm(m):
        lead = m.shape[:-1]
        return m.reshape(lead + (N_KV_HEADS, REP, HEAD_DIM)).swapaxes(-3, -2).reshape(lead + (ATTN_W,))

    w_in_p = jnp.concatenate([attn_perm(w_in[:, :ATTN_W]), w_in[:, ATTN_W:]], axis=1).astype(BF16)
    e = jnp.kron(jnp.eye(E_TILE // HEAD_DIM, dtype=F32),
                 jnp.full((HEAD_DIM, HEAD_DIM), 1.0 / HEAD_DIM, F32)).astype(BF16)
    gqk = jnp.concatenate([jnp.tile(q_norm_g, N_Q_HEADS) * (HEAD_DIM ** -0.5 * LOG2E),
                           jnp.tile(k_norm_g, N_KV_HEADS)])[None, :]

    fill = jnp.full((N_Q_HEADS, KEY_ROWS), NEG_INF, F32).at[:, N_KEYS].set(sinks * LOG2E)

    lam_re = jnp.minimum(ssm_A_re, EIG_CLIP)
    lam_im = ssm_A_im
    dt = jnp.exp(ssm_log_dt)[:, None]
    mag = jnp.exp(lam_re * dt)
    ar = mag * jnp.cos(lam_im * dt)
    ai = mag * jnp.sin(lam_im * dt)
    den = lam_re * lam_re + lam_im * lam_im
    cr = ((ar - 1.0) * lam_re + ai * lam_im) / den
    ci = (ai * lam_re - (ar - 1.0) * lam_im) / den
    bb_re = cr[..., None] * ssm_B_re - ci[..., None] * ssm_B_im
    bb_im = cr[..., None] * ssm_B_im + ci[..., None] * ssm_B_re
    a = jnp.stack([ar.reshape(2, HALF_RI), ai.reshape(2, HALF_RI)], axis=1).reshape(4, HALF_RI)
    eye = jnp.eye(HALF_G, dtype=F32)
    bb = jnp.stack([bb_re, bb_im], 0).reshape(2, 2, HALF_G, SSM_P, SSM_CH)
    bbd = jnp.einsum('ab,rhapc->hacrbp', eye, bb).reshape(2, HALF_U, HALF_STATE).astype(BF16)
    cc = jnp.stack([ssm_C_re, -ssm_C_im], 0).reshape(2, 2, HALF_G, SSM_CH, SSM_P)
    ccat = jnp.einsum('ab,rhacp->hrapbc', eye, cc).reshape(2, HALF_STATE, HALF_U).astype(BF16)

    w_out_p = jnp.concatenate(
        [attn_perm(w_out[:ATTN_W].T).T, w_out[ATTN_W:]], axis=0).astype(BF16)
    return {
        "g1": norm1_g[None, :], "w_in": w_in_p, "e": e, "gqk": gqk, "fill": fill,
        "a": a, "bbd": bbd, "ccat": ccat, "dvec": ssm_D.reshape(1, SSM_W),
        "w_glu": w_glu.astype(BF16), "b_glu": b_glu[None, :],
        "ga": attn_perm(attn_out_g)[None, :], "gs": ssm_out_g[None, :], "w_out": w_out_p,
        "g2": norm2_g[None, :], "w_up": w_up.astype(BF16), "w_down": w_down.astype(BF16),
    }


def _state_to_lanes(re, im):
    nb = re.shape[0]
    return jnp.stack([re.reshape(nb, 2, HALF_RI), im.reshape(nb, 2, HALF_RI)], axis=2).reshape(nb, STATE_W)


def _lanes_to_state(st):
    nb = st.shape[0]
    s = st.reshape(nb, 2, 2, HALF_RI)
    return s[:, :, 0].reshape(nb, SSM_G, SSM_P), s[:, :, 1].reshape(nb, SSM_G, SSM_P)


def kernel(x_prompt, x_sample, cache_swa_k, cache_swa_v, state_ssm_re, state_ssm_im, meta_tokens, norm1_g, w_in, q_norm_g, k_norm_g, sinks, ssm_A_re, ssm_A_im, ssm_log_dt, ssm_B_re, ssm_B_im, ssm_C_re, ssm_C_im, ssm_D, w_glu, b_glu, attn_out_g, ssm_out_g, w_out, norm2_g, w_up, w_down):
    assert norm1_g.shape[0] == 1, "single-layer trunk"
    nb, seq, _ = x_prompt.shape
    nd, dseq, _ = x_sample.shape
    assert dseq == CHUNK and seq % CHUNK == 0 and seq >= WINDOW and nb % BG == 0 and nd % BG == 0
    w = _prep_weights(norm1_g[0], w_in[0], q_norm_g[0], k_norm_g[0], sinks[0], ssm_A_re[0],
                      ssm_A_im[0], ssm_log_dt[0], ssm_B_re[0], ssm_B_im[0], ssm_C_re[0],
                      ssm_C_im[0], ssm_D[0], w_glu[0], b_glu[0], attn_out_g[0], ssm_out_g[0],
                      w_out[0], norm2_g[0], w_up[0], w_down[0])

    k_meta, v_meta, st_meta = _meta_call(meta_tokens, w)
    kmeta_b = jnp.broadcast_to(k_meta[None], (nb, N_META, KV_W))
    vmeta_b = jnp.broadcast_to(v_meta[None], (nb, N_META, KV_W))
    empty_win = ((0, 0), (0, WINDOW), (0, 0))
    dp, kp, vp, stp = _mixer_call(
        x_prompt, jnp.broadcast_to(st_meta, (nb, STATE_W)), jnp.pad(kmeta_b, empty_win),
        jnp.pad(vmeta_b, empty_win), w, chunk_offset=0, n_out=WINDOW // CHUNK)
    y_prompt = _mlp_call(x_prompt.reshape(nb * seq, D_MODEL), dp.reshape(nb * seq, D_MODEL),
                         w).reshape(nb, seq, D_MODEL)

    ck = cache_swa_k[0].reshape(nd, N_META + WINDOW, KV_W)
    cv = cache_swa_v[0].reshape(nd, N_META + WINDOW, KV_W)
    ds, ks, vs, sts = _mixer_call(
        x_sample, _state_to_lanes(state_ssm_re[0], state_ssm_im[0]),
        ck, cv, w, chunk_offset=WINDOW // CHUNK, n_out=1)
    y_sample = _mlp_call(x_sample.reshape(nd * dseq, D_MODEL), ds.reshape(nd * dseq, D_MODEL),
                         w).reshape(nd, dseq, D_MODEL)

    kv5 = lambda t: t.reshape(t.shape[0], t.shape[1], N_KV_HEADS, HEAD_DIM)[None]
    new_k_prompt = kv5(jnp.concatenate([kmeta_b, kp], axis=1))
    new_v_prompt = kv5(jnp.concatenate([vmeta_b, vp], axis=1))
    srp, sip = _lanes_to_state(stp)
    srs, sis = _lanes_to_state(sts)
    return (y_prompt, y_sample, new_k_prompt, new_v_prompt, srp[None], sip[None],
            kv5(ks), kv5(vs), srs[None], sis[None])
```

```python
import functools
import math

import jax
import jax.numpy as jnp
from jax import lax
from jax.experimental import pallas as pl
from jax.experimental.pallas import tpu as pltpu

D_MODEL = 1024
CHUNK = 64
N_META = 16
HEAD_DIM = 64
ATTN_W = 512
N_KV_HEADS = 2
REP = 4
N_Q_HEADS = N_KV_HEADS * REP
KV_W = N_KV_HEADS * HEAD_DIM
WINDOW = 128
SSM_W = 512
SSM_CH = 16
SSM_G = 32
SSM_P = 64
D_FF = 4 * D_MODEL
IN_W = ATTN_W + 2 * KV_W + SSM_W
EPS = 1e-6
EIG_CLIP = -1e-4
NEG_INF = -1e30

SUBLANES = 8
LANES = 128
V7X_VMEM_BYTES = 64 * 1024 * 1024
COMPILER_VMEM_BYTES = 12 * 1024 * 1024

BG = SUBLANES
ROWS = BG * CHUNK
QK_W = ATTN_W + KV_W
V_OFF = QK_W
U_OFF = QK_W + KV_W
N_KEYS = N_META + WINDOW + CHUNK
KEY_ROWS = 256
LOG2E = 1.4426950408889634
N_SLOTS = WINDOW // CHUNK + 1
HALF_G = SSM_G // 2
HALF_RI = HALF_G * SSM_P
HALF_STATE = 2 * HALF_RI
STATE_W = 2 * HALF_STATE
HALF_U = HALF_G * SSM_CH
U_SLABS = SSM_W // LANES
SCAN_W = 512
MLP_ROWS = 1024
MLP_SUB_ROWS = 256
E_TILE = 256
MXU_N = 256
FF_BLK = 1024

BF16 = jnp.bfloat16
F32 = jnp.float32

_STEP_ORDER = (
    "unpack norm GLU_OUT AN BU:0 S:0 BU:1 SCAN:0 S:1 OP:0 PV:0 S:2 OP:1 SCAN:1 PV:1 S:3 OP:2 PV:2 S:4 OP:3 "
    "PV:3 S:5 W:0 PV:4 S:6 W:1 PV:5 S:7 W:2 PV:6 QKN C:0 PV:7 W:3 C:1 W:4 GLU_IN")


def _rms(x, g):
    ms = jnp.mean(x * x, axis=-1, keepdims=True)
    return x * lax.rsqrt(ms + EPS) * g


def _project(x, g1_ref, win_ref, e_ref, gqk_ref):
    xn = _rms(x, g1_ref[...]).astype(BF16)
    z = jnp.dot(xn, win_ref[...], preferred_element_type=F32)
    zqk = z[:, :QK_W]
    sq = (zqk * zqk).astype(BF16)
    ms = jnp.concatenate(
        [jnp.dot(sq[:, o:o + n], e_ref[:n, :n], preferred_element_type=F32)
         for o, n in ((0, E_TILE), (E_TILE, E_TILE), (2 * E_TILE, QK_W - 2 * E_TILE))], axis=1)
    qk = zqk * lax.rsqrt(ms + EPS) * gqk_ref[...]
    return qk, z


def _meta_kernel(meta_ref, g1_ref, win_ref, e_ref, gqk_ref, a_ref, bbd_ref,
                 k_ref, v_ref, st_ref, bu_scr):
    qk, z = _project(meta_ref[...], g1_ref, win_ref, e_ref, gqk_ref)
    k_ref[...] = qk[:, ATTN_W:]
    v_ref[...] = z[:, V_OFF:U_OFF]
    u = z[:, U_OFF:].astype(BF16)
    for h in range(2):
        bu_scr[:, h * HALF_STATE:(h + 1) * HALF_STATE] = jnp.dot(
            u[:, h * HALF_U:(h + 1) * HALF_U], bbd_ref[h], preferred_element_type=F32)
    for h in range(2):
        ar = a_ref[2 * h:2 * h + 1, :]
        ai = a_ref[2 * h + 1:2 * h + 2, :]
        cr = h * HALF_STATE
        ci = cr + HALF_RI
        xr = jnp.zeros((1, HALF_RI), F32)
        xi = jnp.zeros((1, HALF_RI), F32)
        for t in range(N_META):
            br = bu_scr[t:t + 1, cr:cr + HALF_RI]
            bi = bu_scr[t:t + 1, ci:ci + HALF_RI]
            xr, xi = ar * xr - ai * xi + br, ar * xi + ai * xr + bi
        st_ref[:, cr:cr + HALF_RI] = xr
        st_ref[:, ci:ci + HALF_RI] = xi


def _mixer_kernel(x_ref, st0_ref, k0_ref, v0_ref,
                  g1_ref, win_ref, e_ref, gqk_ref, fill_ref, row0_ref, a_ref, bbd_ref, ccat_ref, dvec_ref,
                  wglu_ref, bglu_ref, ga_ref, gs_ref, wout_ref,
                  d_ref, kout_ref, vout_ref, st_ref,
                  z_scr, xn_scr, qs_scr, du_scr, attn_scr, utb_scr, bu_scr, ytb_scr, ybt_scr, kwin, vwin,
                  merged_scr, *, n_chunks, n_bg, chunk_offset, n_out):
    s = pl.program_id(0)
    n_steps = n_bg * n_chunks
    c = _mix_chunk(s, n_steps) % n_chunks
    live = jnp.logical_and(s >= 1, s <= n_steps)
    gc = c + chunk_offset
    row0 = row0_ref[0]

    @pl.when(s == 0)
    def _():
        z_scr[...] = jnp.zeros((ROWS, IN_W), F32)
        ybt_scr[...] = jnp.zeros((ROWS, SSM_W), F32)
        attn_scr[...] = jnp.zeros((ROWS, ATTN_W), F32)

    @pl.when(jnp.logical_and(c == 0, s <= n_steps))
    def _():
        st_ref[...] = st0_ref[...]
        kwin[:, 0:N_META + WINDOW, :] = k0_ref[...].astype(BF16)
        vwin[:, 0:N_META + WINDOW, :] = v0_ref[...].astype(BF16)
        kwin[:, N_META + WINDOW:, :] = jnp.zeros((BG, KEY_ROWS - N_META - WINDOW, KV_W), BF16)
        vwin[:, N_META + WINDOW:, :] = jnp.zeros((BG, KEY_ROWS - N_META - WINDOW, KV_W), BF16)

    lane = lax.broadcasted_iota(jnp.int32, (CHUNK, LANES), 1)
    head_lanes = (lane < HEAD_DIM, lane >= HEAD_DIM)
    key_limit = jnp.where(gc >= N_SLOTS - 1, N_KEYS, N_META + CHUNK * (gc + 1))
    key_ok = lax.broadcasted_iota(jnp.int32, (1, KEY_ROWS), 1) < key_limit

    def unpack():
        k = z_scr[:, ATTN_W:QK_W].reshape(BG, CHUNK, KV_W)
        v = z_scr[:, V_OFF:U_OFF].reshape(BG, CHUNK, KV_W)
        u = z_scr[:, U_OFF:]
        slot_row = pl.multiple_of(N_META + lax.rem(gc, N_SLOTS) * CHUNK, 16)
        kwin[:, pl.ds(slot_row, CHUNK), :] = k.astype(BF16)
        vwin[:, pl.ds(slot_row, CHUNK), :] = v.astype(BF16)
        out_row = pl.multiple_of(lax.rem(c, n_out) * CHUNK, CHUNK)
        kout_ref[:, pl.ds(out_row, CHUNK), :] = k
        vout_ref[:, pl.ds(out_row, CHUNK), :] = v
        for b in range(BG):
            qb = z_scr[b * CHUNK:(b + 1) * CHUNK, 0:ATTN_W]
            parts = [jnp.where(head_lanes[kvh], qb[:, r * LANES:(r + 1) * LANES], 0.0)
                     for kvh in range(N_KV_HEADS) for r in range(REP)]
            qs_scr[b] = jnp.concatenate(parts, axis=0).astype(BF16)
        for b in range(BG):
            for j in range(U_SLABS):
                utb_scr[j, pl.ds(b, CHUNK, stride=BG), :] = (
                    u[b * CHUNK:(b + 1) * CHUNK, j * LANES:(j + 1) * LANES])
        du_scr[...] = dvec_ref[...] * u

    def out_proj(j):
        cols = slice(j * MXU_N, (j + 1) * MXU_N)
        d_ref[:, :, cols] = jnp.dot(merged_scr[...], wout_ref[:, cols],
                                    preferred_element_type=F32).reshape(BG, CHUNK, MXU_N)

    def proj_norm():
        xn_scr[...] = _rms(x_ref[...].reshape(ROWS, D_MODEL), g1_ref[...]).astype(BF16)

    def proj(j):
        cols = slice(j * MXU_N, (j + 1) * MXU_N)
        z_scr[:, cols] = jnp.dot(xn_scr[...], win_ref[:, cols], preferred_element_type=F32)

    def proj_qknorm():
        for o, n in ((0, E_TILE), (E_TILE, E_TILE), (2 * E_TILE, QK_W - 2 * E_TILE)):
            zq = z_scr[:, o:o + n]
            ms = jnp.dot((zq * zq).astype(BF16), e_ref[:n, :n], preferred_element_type=F32)
            z_scr[:, o:o + n] = zq * lax.rsqrt(ms + EPS) * gqk_ref[:, o:o + n]

    probs = {}

    def attn_scores(b):
        sc = lax.dot_general(qs_scr[b], kwin[b], (((1,), (1,)), ((), ())), preferred_element_type=F32)
        p_parts = []
        inv_l = []
        for i in range(N_Q_HEADS):
            sb = jnp.where(key_ok, sc[i * CHUNK:(i + 1) * CHUNK], fill_ref[i:i + 1, :])
            m = jnp.max(sb, axis=-1, keepdims=True)
            p = jnp.exp2(sb - m)
            l = jnp.sum(p, axis=-1, keepdims=True)
            p_parts.append(p.astype(BF16))
            inv_l.append(1.0 / l)
        probs[b] = (jnp.concatenate(p_parts, axis=0), inv_l)

    def attn_values(b):
        p_all, inv_l = probs.pop(b)
        o = jnp.dot(p_all, vwin[b], preferred_element_type=F32)
        outs = []
        for r in range(REP):
            o0 = o[r * CHUNK:(r + 1) * CHUNK] * inv_l[r]
            o1 = o[(REP + r) * CHUNK:(REP + r + 1) * CHUNK] * inv_l[REP + r]
            outs.append(jnp.where(head_lanes[0], o0, o1))
        attn_scr[b * CHUNK:(b + 1) * CHUNK, :] = jnp.concatenate(outs, axis=1)

    def s5_in(h):
        uh = jnp.concatenate([utb_scr[2 * h], utb_scr[2 * h + 1]], axis=1).astype(BF16)
        bu_scr[:, h * HALF_STATE:(h + 1) * HALF_STATE] = jnp.dot(
            uh, bbd_ref[h], preferred_element_type=F32)

    def s5_scan(h):
        for w in range(HALF_RI // SCAN_W):
            cr = h * HALF_STATE + w * SCAN_W
            ci = cr + HALF_RI
            ar = jnp.broadcast_to(a_ref[2 * h:2 * h + 1, w * SCAN_W:(w + 1) * SCAN_W], (BG, SCAN_W))
            ai = jnp.broadcast_to(a_ref[2 * h + 1:2 * h + 2, w * SCAN_W:(w + 1) * SCAN_W], (BG, SCAN_W))

            def scan_body(t, carry, cr=cr, ci=ci, ar=ar, ai=ai):
                xr, xi = carry
                r = pl.multiple_of(t * BG, BG)
                rl = pl.multiple_of(row0 + t * BG, BG)
                br = bu_scr[pl.ds(rl, BG), cr:cr + SCAN_W]
                bi = bu_scr[pl.ds(rl, BG), ci:ci + SCAN_W]
                nr = ar * xr - ai * xi + br
                ni = ar * xi + ai * xr + bi
                bu_scr[pl.ds(r, BG), cr:cr + SCAN_W] = nr
                bu_scr[pl.ds(r, BG), ci:ci + SCAN_W] = ni
                return nr, ni

            xr0 = st_ref[:, cr:cr + SCAN_W]
            xi0 = st_ref[:, ci:ci + SCAN_W]
            xr, xi = lax.fori_loop(0, CHUNK, scan_body, (xr0, xi0), unroll=True)
            st_ref[:, cr:cr + SCAN_W] = jnp.where(live, xr, xr0)
            st_ref[:, ci:ci + SCAN_W] = jnp.where(live, xi, xi0)

    def s5_out(h):
        xs = bu_scr[pl.ds(pl.multiple_of(row0, ROWS), ROWS), h * HALF_STATE:(h + 1) * HALF_STATE]
        yh = jnp.dot(xs.astype(BF16), ccat_ref[h],
                     preferred_element_type=F32)
        for jj in range(2):
            ytb_scr[2 * h + jj] = yh[:, jj * LANES:(jj + 1) * LANES]

    def glu_in():
        for b in range(BG):
            for j in range(U_SLABS):
                ybt_scr[b * CHUNK:(b + 1) * CHUNK, j * LANES:(j + 1) * LANES] = (
                    ytb_scr[j, pl.ds(b, CHUNK, stride=BG), :])
        ybt_scr[...] = jax.nn.gelu(ybt_scr[...] + du_scr[...])

    def glu_out():
        zs = ybt_scr[...]
        gate = jnp.dot(zs.astype(BF16), wglu_ref[...], preferred_element_type=F32) + bglu_ref[...]
        s_out = zs * jax.nn.sigmoid(gate)
        merged_scr[:, ATTN_W:] = _rms(s_out, gs_ref[...]).astype(BF16)

    def attn_norm():
        merged_scr[:, :ATTN_W] = _rms(attn_scr[...], ga_ref[...]).astype(BF16)

    pieces = {"unpack": unpack, "norm": proj_norm, "W": proj, "QKN": proj_qknorm, "OP": out_proj,
              "S": attn_scores, "PV": attn_values, "BU": s5_in, "SCAN": s5_scan, "C": s5_out,
              "GLU_IN": glu_in, "GLU_OUT": glu_out, "AN": attn_norm}
    for item in _STEP_ORDER.split():
        name, _, arg = item.partition(":")
        pieces[name](*([int(arg)] if arg else []))


def _mlp_kernel(x_ref, d_ref, g2_ref, wup_ref, wdown_ref, o_ref):
    for r in range(0, MLP_ROWS, MLP_SUB_ROWS):
        rows = slice(r, r + MLP_SUB_ROWS)
        h = x_ref[rows, :] + d_ref[rows, :]
        xn = _rms(h, g2_ref[...]).astype(BF16)
        acc = h
        for j in range(D_FF // FF_BLK):
            f = jnp.dot(xn, wup_ref[:, j * FF_BLK:(j + 1) * FF_BLK], preferred_element_type=F32)
            f = jnp.maximum(f, 0.0)
            acc = acc + jnp.dot((f * f).astype(BF16), wdown_ref[j * FF_BLK:(j + 1) * FF_BLK, :],
                                preferred_element_type=F32)
        o_ref[rows, :] = acc


def _mix_chunk(s, n_steps):
    return jnp.clip(s - 1, 0, n_steps - 1)


def _const_spec(shape):
    n = len(shape)
    return pl.BlockSpec(shape, lambda *_: (0,) * n, pipeline_mode=pl.Buffered(1))


def _vmem_limit(blocks, scratch=()):
    need = COMPILER_VMEM_BYTES
    need += sum(n_buf * math.prod(shape) * jnp.dtype(dtype).itemsize for shape, dtype, n_buf in blocks)
    need += sum(math.prod(m.shape) * jnp.dtype(m.dtype).itemsize for m in scratch)
    assert need <= V7X_VMEM_BYTES, need
    return need


def _spec_blocks(specs, arrays):
    return [(spec.block_shape, a.dtype, spec.pipeline_mode.buffer_count if spec.pipeline_mode else 2)
            for spec, a in zip(specs, arrays) if spec.block_shape is not None]


def _meta_call(meta, w):
    args = (meta, w["g1"], w["w_in"], w["e"], w["gqk"], w["a"], w["bbd"])
    out_shape = (jax.ShapeDtypeStruct((N_META, KV_W), F32),
                 jax.ShapeDtypeStruct((N_META, KV_W), F32),
                 jax.ShapeDtypeStruct((1, STATE_W), F32))
    scratch = [pltpu.VMEM((N_META, STATE_W), F32)]
    whole = [(a.shape, a.dtype, 1) for a in args + out_shape]
    return pl.pallas_call(
        _meta_kernel, out_shape=out_shape, scratch_shapes=scratch,
        compiler_params=pltpu.CompilerParams(vmem_limit_bytes=_vmem_limit(whole, scratch)),
        name="meta",
    )(*args)


def _mixer_call(x, st0, k0, v0, w, *, chunk_offset, n_out):
    nb, s_len, _ = x.shape
    n_chunks = s_len // CHUNK
    n_bg = nb // BG
    n_steps = n_bg * n_chunks
    assert n_chunks % n_out == 0
    grid = (n_steps + 2,)

    def chunk_block(gidx):
        return (gidx // n_chunks, gidx % n_chunks, 0)

    x_index = lambda s: chunk_block(jnp.minimum(s, n_steps - 1))
    d_index = lambda s: chunk_block(jnp.clip(s - 2, 0, n_steps - 1))
    per_bg = lambda s: (_mix_chunk(s, n_steps) // n_chunks, 0, 0)
    per_bg2 = lambda s: (_mix_chunk(s, n_steps) // n_chunks, 0)
    in_specs = [
        pl.BlockSpec((BG, CHUNK, D_MODEL), x_index),
        pl.BlockSpec((BG, STATE_W), per_bg2),
        pl.BlockSpec((BG, N_META + WINDOW, KV_W), per_bg),
        pl.BlockSpec((BG, N_META + WINDOW, KV_W), per_bg),
        _const_spec((1, D_MODEL)),
        _const_spec((D_MODEL, IN_W)),
        _const_spec((E_TILE, E_TILE)),
        _const_spec((1, QK_W)),
        _const_spec((N_Q_HEADS, KEY_ROWS)),
        pl.BlockSpec(memory_space=pltpu.SMEM),
        _const_spec((4, HALF_RI)),
        _const_spec((2, HALF_U, HALF_STATE)),
        _const_spec((2, HALF_STATE, HALF_U)),
        _const_spec((1, SSM_W)),
        _const_spec((SSM_W, SSM_W)),
        _const_spec((1, SSM_W)),
        _const_spec((1, ATTN_W)),
        _const_spec((1, SSM_W)),
        _const_spec((D_MODEL, D_MODEL)),
    ]
    out_specs = (
        pl.BlockSpec((BG, CHUNK, D_MODEL), d_index),
        pl.BlockSpec((BG, n_out * CHUNK, KV_W), per_bg),
        pl.BlockSpec((BG, n_out * CHUNK, KV_W), per_bg),
        pl.BlockSpec((BG, STATE_W), per_bg2),
    )
    out_shape = (
        jax.ShapeDtypeStruct((nb, s_len, D_MODEL), F32),
        jax.ShapeDtypeStruct((nb, n_out * CHUNK, KV_W), F32),
        jax.ShapeDtypeStruct((nb, n_out * CHUNK, KV_W), F32),
        jax.ShapeDtypeStruct((nb, STATE_W), F32),
    )
    scratch = [
        pltpu.VMEM((ROWS, IN_W), F32),
        pltpu.VMEM((ROWS, D_MODEL), BF16),
        pltpu.VMEM((BG, N_Q_HEADS * CHUNK, LANES), BF16),
        pltpu.VMEM((ROWS, SSM_W), F32),
        pltpu.VMEM((ROWS, ATTN_W), F32),
        pltpu.VMEM((U_SLABS, ROWS, LANES), F32),
        pltpu.VMEM((ROWS, STATE_W), F32),
        pltpu.VMEM((U_SLABS, ROWS, LANES), F32),
        pltpu.VMEM((ROWS, SSM_W), F32),
        pltpu.VMEM((BG, KEY_ROWS, KV_W), BF16),
        pltpu.VMEM((BG, KEY_ROWS, KV_W), BF16),
        pltpu.VMEM((ROWS, D_MODEL), BF16),
    ]
    kern = functools.partial(_mixer_kernel, n_chunks=n_chunks, n_bg=n_bg,
                             chunk_offset=chunk_offset, n_out=n_out)
    args = (x, st0, k0, v0,
            w["g1"], w["w_in"], w["e"], w["gqk"], w["fill"], jnp.zeros((1,), jnp.int32),
            w["a"], w["bbd"], w["ccat"], w["dvec"],
            w["w_glu"], w["b_glu"], w["ga"], w["gs"], w["w_out"])
    blocks = _spec_blocks(in_specs, args) + _spec_blocks(out_specs, out_shape)
    return pl.pallas_call(
        kern, grid=grid, in_specs=in_specs, out_specs=out_specs, out_shape=out_shape,
        scratch_shapes=scratch,
        compiler_params=pltpu.CompilerParams(
            dimension_semantics=("arbitrary",), vmem_limit_bytes=_vmem_limit(blocks, scratch)),
        name="mixer",
    )(*args)


def _mlp_call(x, d, w):
    n = x.shape[0]
    rows = pl.BlockSpec((MLP_ROWS, D_MODEL), lambda i: (i, 0))
    in_specs = [rows, rows, _const_spec((1, D_MODEL)), _const_spec((D_MODEL, D_FF)),
                _const_spec((D_FF, D_MODEL))]
    args = (x, d, w["g2"], w["w_up"], w["w_down"])
    out_shape = jax.ShapeDtypeStruct((n, D_MODEL), F32)
    blocks = _spec_blocks(in_specs, args) + _spec_blocks([rows], [out_shape])
    return pl.pallas_call(
        _mlp_kernel, grid=(n // MLP_ROWS,), in_specs=in_specs, out_specs=rows, out_shape=out_shape,
        compiler_params=pltpu.CompilerParams(
            dimension_semantics=("arbitrary",), vmem_limit_bytes=_vmem_limit(blocks)),
        name="mlp",
    )(*args)


def _prep_weights(norm1_g, w_in, q_norm_g, k_norm_g, sinks, ssm_A_re, ssm_A_im, ssm_log_dt,
                  ssm_B_re, ssm_B_im, ssm_C_re, ssm_C_im, ssm_D, w_glu, b_glu, attn_out_g,
                  ssm_out_g, w_out, norm2_g, w_up, w_down):
    def attn_perm(m):
        lead = m.shape[:-1]
        return m.reshape(lead + (N_KV_HEADS, REP, HEAD_DIM)).swapaxes(-3, -2).reshape(lead + (ATTN_W,))

    w_in_p = jnp.concatenate([attn_perm(w_in[:, :ATTN_W]), w_in[:, ATTN_W:]], axis=1).astype(BF16)
    e = jnp.kron(jnp.eye(E_TILE // HEAD_DIM, dtype=F32),
                 jnp.full((HEAD_DIM, HEAD_DIM), 1.0 / HEAD_DIM, F32)).astype(BF16)
    gqk = jnp.concatenate([jnp.tile(q_norm_g, N_Q_HEADS) * (HEAD_DIM ** -0.5 * LOG2E),
                           jnp.tile(k_norm_g, N_KV_HEADS)])[None, :]

    fill = jnp.full((N_Q_HEADS, KEY_ROWS), NEG_INF, F32).at[:, N_KEYS].set(sinks * LOG2E)

    lam_re = jnp.minimum(ssm_A_re, EIG_CLIP)
    lam_im = ssm_A_im
    dt = jnp.exp(ssm_log_dt)[:, None]
    mag = jnp.exp(lam_re * dt)
    ar = mag * jnp.cos(lam_im * dt)
    ai = mag * jnp.sin(lam_im * dt)
    den = lam_re * lam_re + lam_im * lam_im
    cr = ((ar - 1.0) * lam_re + ai * lam_im) / den
    ci = (ai * lam_re - (ar - 1.0) * lam_im) / den
    bb_re = cr[..., None] * ssm_B_re - ci[..., None] * ssm_B_im
    bb_im = cr[..., None] * ssm_B_im + ci[..., None] * ssm_B_re
    a = jnp.stack([ar.reshape(2, HALF_RI), ai.reshape(2, HALF_RI)], axis=1).reshape(4, HALF_RI)
    eye = jnp.eye(HALF_G, dtype=F32)
    bb = jnp.stack([bb_re, bb_im], 0).reshape(2, 2, HALF_G, SSM_P, SSM_CH)
    bbd = jnp.einsum('ab,rhapc->hacrbp', eye, bb).reshape(2, HALF_U, HALF_STATE).astype(BF16)
    cc = jnp.stack([ssm_C_re, -ssm_C_im], 0).reshape(2, 2, HALF_G, SSM_CH, SSM_P)
    ccat = jnp.einsum('ab,rhacp->hrapbc', eye, cc).reshape(2, HALF_STATE, HALF_U).astype(BF16)

    w_out_p = jnp.concatenate(
        [attn_perm(w_out[:ATTN_W].T).T, w_out[ATTN_W:]], axis=0).astype(BF16)
    return {
        "g1": norm1_g[None, :], "w_in": w_in_p, "e": e, "gqk": gqk, "fill": fill,
        "a": a, "bbd": bbd, "ccat": ccat, "dvec": ssm_D.reshape(1, SSM_W),
        "w_glu": w_glu.astype(BF16), "b_glu": b_glu[None, :],
        "ga": attn_perm(attn_out_g)[None, :], "gs": ssm_out_g[None, :], "w_out": w_out_p,
        "g2": norm2_g[None, :], "w_up": w_up.astype(BF16), "w_down": w_down.astype(BF16),
    }


def _state_to_lanes(re, im):
    nb = re.shape[0]
    return jnp.stack([re.reshape(nb, 2, HALF_RI), im.reshape(nb, 2, HALF_RI)], axis=2).reshape(nb, STATE_W)


def _lanes_to_state(st):
    nb = st.shape[0]
    s = st.reshape(nb, 2, 2, HALF_RI)
    return s[:, :, 0].reshape(nb, SSM_G, SSM_P), s[:, :, 1].reshape(nb, SSM_G, SSM_P)


def kernel(x_prompt, x_sample, cache_swa_k, cache_swa_v, state_ssm_re, state_ssm_im, meta_tokens, norm1_g, w_in, q_norm_g, k_norm_g, sinks, ssm_A_re, ssm_A_im, ssm_log_dt, ssm_B_re, ssm_B_im, ssm_C_re, ssm_C_im, ssm_D, w_glu, b_glu, attn_out_g, ssm_out_g, w_out, norm2_g, w_up, w_down):
    assert norm1_g.shape[0] == 1, "single-layer trunk"
    nb, seq, _ = x_prompt.shape
    nd, dseq, _ = x_sample.shape
    assert dseq == CHUNK and seq % CHUNK == 0 and seq >= WINDOW and nb % BG == 0 and nd % BG == 0
    w = _prep_weights(norm1_g[0], w_in[0], q_norm_g[0], k_norm_g[0], sinks[0], ssm_A_re[0],
                      ssm_A_im[0], ssm_log_dt[0], ssm_B_re[0], ssm_B_im[0], ssm_C_re[0],
                      ssm_C_im[0], ssm_D[0], w_glu[0], b_glu[0], attn_out_g[0], ssm_out_g[0],
                      w_out[0], norm2_g[0], w_up[0], w_down[0])

    k_meta, v_meta, st_meta = _meta_call(meta_tokens, w)
    kmeta_b = jnp.broadcast_to(k_meta[None], (nb, N_META, KV_W))
    vmeta_b = jnp.broadcast_to(v_meta[None], (nb, N_META, KV_W))
    empty_win = ((0, 0), (0, WINDOW), (0, 0))
    dp, kp, vp, stp = _mixer_call(
        x_prompt, jnp.broadcast_to(st_meta, (nb, STATE_W)), jnp.pad(kmeta_b, empty_win),
        jnp.pad(vmeta_b, empty_win), w, chunk_offset=0, n_out=WINDOW // CHUNK)
    y_prompt = _mlp_call(x_prompt.reshape(nb * seq, D_MODEL), dp.reshape(nb * seq, D_MODEL),
                         w).reshape(nb, seq, D_MODEL)

    ck = cache_swa_k[0].reshape(nd, N_META + WINDOW, KV_W)
    cv = cache_swa_v[0].reshape(nd, N_META + WINDOW, KV_W)
    ds, ks, vs, sts = _mixer_call(
        x_sample, _state_to_lanes(state_ssm_re[0], state_ssm_im[0]),
        ck, cv, w, chunk_offset=WINDOW // CHUNK, n_out=1)
    y_sample = _mlp_call(x_sample.reshape(nd * dseq, D_MODEL), ds.reshape(nd * dseq, D_MODEL),
                         w).reshape(nd, dseq, D_MODEL)

    kv5 = lambda t: t.reshape(t.shape[0], t.shape[1], N_KV_HEADS, HEAD_DIM)[None]
    new_k_prompt = kv5(jnp.concatenate([kmeta_b, kp], axis=1))
    new_v_prompt = kv5(jnp.concatenate([vmeta_b, vp], axis=1))
    srp, sip = _lanes_to_state(stp)
    srs, sis = _lanes_to_state(sts)
    return (y_prompt, y_sample, new_k_prompt, new_v_prompt, srp[None], sip[None],
            kv5(ks), kv5(vs), srs[None], sis[None])
```

```python
import functools
import math

import jax
import jax.numpy as jnp
from jax import lax
from jax.experimental import pallas as pl
from jax.experimental.pallas import tpu as pltpu

D_MODEL = 1024
CHUNK = 64
N_META = 16
HEAD_DIM = 64
ATTN_W = 512
N_KV_HEADS = 2
REP = 4
N_Q_HEADS = N_KV_HEADS * REP
KV_W = N_KV_HEADS * HEAD_DIM
WINDOW = 128
SSM_W = 512
SSM_CH = 16
SSM_G = 32
SSM_P = 64
D_FF = 4 * D_MODEL
IN_W = ATTN_W + 2 * KV_W + SSM_W
EPS = 1e-6
EIG_CLIP = -1e-4
NEG_INF = -1e30

SUBLANES = 8
LANES = 128
V7X_VMEM_BYTES = 64 * 1024 * 1024
COMPILER_VMEM_BYTES = 12 * 1024 * 1024

BG = SUBLANES
ROWS = BG * CHUNK
QK_W = ATTN_W + KV_W
V_OFF = QK_W
U_OFF = QK_W + KV_W
N_KEYS = N_META + WINDOW + CHUNK
KEY_ROWS = 256
LOG2E = 1.4426950408889634
N_SLOTS = WINDOW // CHUNK + 1
HALF_G = SSM_G // 2
HALF_RI = HALF_G * SSM_P
HALF_STATE = 2 * HALF_RI
STATE_W = 2 * HALF_STATE
HALF_U = HALF_G * SSM_CH
U_SLABS = SSM_W // LANES
SCAN_W = 512
MLP_ROWS = 1024
MLP_SUB_ROWS = 256
E_TILE = 256
MXU_N = 256
FF_BLK = 1024

BF16 = jnp.bfloat16
F32 = jnp.float32

_STEP_ORDER = (
    "unpack norm GLU_OUT AN BU:0 S:0 BU:1 SCAN:0 S:1 OP:0 PV:0 S:2 OP:1 SCAN:1 PV:1 S:3 OP:2 PV:2 S:4 OP:3 "
    "PV:3 S:5 W:0 PV:4 S:6 W:1 PV:5 S:7 W:2 W:3 PV:6 QKN C:0 PV:7 C:1 W:4 GLU_IN")


def _rms(x, g):
    ms = jnp.mean(x * x, axis=-1, keepdims=True)
    return x * lax.rsqrt(ms + EPS) * g


def _project(x, g1_ref, win_ref, e_ref, gqk_ref):
    xn = _rms(x, g1_ref[...]).astype(BF16)
    z = jnp.dot(xn, win_ref[...], preferred_element_type=F32)
    zqk = z[:, :QK_W]
    sq = (zqk * zqk).astype(BF16)
    ms = jnp.concatenate(
        [jnp.dot(sq[:, o:o + n], e_ref[:n, :n], preferred_element_type=F32)
         for o, n in ((0, E_TILE), (E_TILE, E_TILE), (2 * E_TILE, QK_W - 2 * E_TILE))], axis=1)
    qk = zqk * lax.rsqrt(ms + EPS) * gqk_ref[...]
    return qk, z


def _meta_kernel(meta_ref, g1_ref, win_ref, e_ref, gqk_ref, a_ref, bbd_ref,
                 k_ref, v_ref, st_ref, bu_scr):
    qk, z = _project(meta_ref[...], g1_ref, win_ref, e_ref, gqk_ref)
    k_ref[...] = qk[:, ATTN_W:]
    v_ref[...] = z[:, V_OFF:U_OFF]
    u = z[:, U_OFF:].astype(BF16)
    for h in range(2):
        bu_scr[:, h * HALF_STATE:(h + 1) * HALF_STATE] = jnp.dot(
            u[:, h * HALF_U:(h + 1) * HALF_U], bbd_ref[h], preferred_element_type=F32)
    for h in range(2):
        ar = a_ref[2 * h:2 * h + 1, :]
        ai = a_ref[2 * h + 1:2 * h + 2, :]
        cr = h * HALF_STATE
        ci = cr + HALF_RI
        xr = jnp.zeros((1, HALF_RI), F32)
        xi = jnp.zeros((1, HALF_RI), F32)
        for t in range(N_META):
            br = bu_scr[t:t + 1, cr:cr + HALF_RI]
            bi = bu_scr[t:t + 1, ci:ci + HALF_RI]
            xr, xi = ar * xr - ai * xi + br, ar * xi + ai * xr + bi
        st_ref[:, cr:cr + HALF_RI] = xr
        st_ref[:, ci:ci + HALF_RI] = xi


def _mixer_kernel(x_ref, st0_ref, k0_ref, v0_ref,
                  g1_ref, win_ref, e_ref, gqk_ref, fill_ref, row0_ref, a_ref, bbd_ref, ccat_ref, dvec_ref,
                  wglu_ref, bglu_ref, ga_ref, gs_ref, wout_ref,
                  d_ref, kout_ref, vout_ref, st_ref,
                  z_scr, xn_scr, qs_scr, du_scr, attn_scr, utb_scr, bu_scr, ytb_scr, ybt_scr, kwin, vwin,
                  merged_scr, *, n_chunks, n_bg, chunk_offset, n_out):
    s = pl.program_id(0)
    n_steps = n_bg * n_chunks
    c = _mix_chunk(s, n_steps) % n_chunks
    live = jnp.logical_and(s >= 1, s <= n_steps)
    gc = c + chunk_offset
    row0 = row0_ref[0]

    @pl.when(s == 0)
    def _():
        z_scr[...] = jnp.zeros((ROWS, IN_W), F32)
        ybt_scr[...] = jnp.zeros((ROWS, SSM_W), F32)
        attn_scr[...] = jnp.zeros((ROWS, ATTN_W), F32)

    @pl.when(jnp.logical_and(c == 0, s <= n_steps))
    def _():
        st_ref[...] = st0_ref[...]
        kwin[:, 0:N_META + WINDOW, :] = k0_ref[...].astype(BF16)
        vwin[:, 0:N_META + WINDOW, :] = v0_ref[...].astype(BF16)
        kwin[:, N_META + WINDOW:, :] = jnp.zeros((BG, KEY_ROWS - N_META - WINDOW, KV_W), BF16)
        vwin[:, N_META + WINDOW:, :] = jnp.zeros((BG, KEY_ROWS - N_META - WINDOW, KV_W), BF16)

    lane = lax.broadcasted_iota(jnp.int32, (CHUNK, LANES), 1)
    head_lanes = (lane < HEAD_DIM, lane >= HEAD_DIM)
    key_limit = jnp.where(gc >= N_SLOTS - 1, N_KEYS, N_META + CHUNK * (gc + 1))
    key_ok = lax.broadcasted_iota(jnp.int32, (1, KEY_ROWS), 1) < key_limit

    def unpack():
        k = z_scr[:, ATTN_W:QK_W].reshape(BG, CHUNK, KV_W)
        v = z_scr[:, V_OFF:U_OFF].reshape(BG, CHUNK, KV_W)
        u = z_scr[:, U_OFF:]
        slot_row = pl.multiple_of(N_META + lax.rem(gc, N_SLOTS) * CHUNK, 16)
        kwin[:, pl.ds(slot_row, CHUNK), :] = k.astype(BF16)
        vwin[:, pl.ds(slot_row, CHUNK), :] = v.astype(BF16)
        out_row = pl.multiple_of(lax.rem(c, n_out) * CHUNK, CHUNK)
        kout_ref[:, pl.ds(out_row, CHUNK), :] = k
        vout_ref[:, pl.ds(out_row, CHUNK), :] = v
        for b in range(BG):
            qb = z_scr[b * CHUNK:(b + 1) * CHUNK, 0:ATTN_W]
            parts = [jnp.where(head_lanes[kvh], qb[:, r * LANES:(r + 1) * LANES], 0.0)
                     for kvh in range(N_KV_HEADS) for r in range(REP)]
            qs_scr[b] = jnp.concatenate(parts, axis=0).astype(BF16)
        for b in range(BG):
            for j in range(U_SLABS):
                utb_scr[j, pl.ds(b, CHUNK, stride=BG), :] = (
                    u[b * CHUNK:(b + 1) * CHUNK, j * LANES:(j + 1) * LANES])
        du_scr[...] = dvec_ref[...] * u

    def out_proj(j):
        cols = slice(j * MXU_N, (j + 1) * MXU_N)
        d_ref[:, :, cols] = jnp.dot(merged_scr[...], wout_ref[:, cols],
                                    preferred_element_type=F32).reshape(BG, CHUNK, MXU_N)

    def proj_norm():
        xn_scr[...] = _rms(x_ref[...].reshape(ROWS, D_MODEL), g1_ref[...]).astype(BF16)

    def proj(j):
        cols = slice(j * MXU_N, (j + 1) * MXU_N)
        z_scr[:, cols] = jnp.dot(xn_scr[...], win_ref[:, cols], preferred_element_type=F32)

    def proj_qknorm():
        for o, n in ((0, E_TILE), (E_TILE, E_TILE), (2 * E_TILE, QK_W - 2 * E_TILE)):
            zq = z_scr[:, o:o + n]
            ms = jnp.dot((zq * zq).astype(BF16), e_ref[:n, :n], preferred_element_type=F32)
            z_scr[:, o:o + n] = zq * lax.rsqrt(ms + EPS) * gqk_ref[:, o:o + n]

    probs = {}

    def attn_scores(b):
        sc = lax.dot_general(qs_scr[b], kwin[b], (((1,), (1,)), ((), ())), preferred_element_type=F32)
        p_parts = []
        inv_l = []
        for i in range(N_Q_HEADS):
            sb = jnp.where(key_ok, sc[i * CHUNK:(i + 1) * CHUNK], fill_ref[i:i + 1, :])
            m = jnp.max(sb, axis=-1, keepdims=True)
            p = jnp.exp2(sb - m)
            l = jnp.sum(p, axis=-1, keepdims=True)
            p_parts.append(p.astype(BF16))
            inv_l.append(1.0 / l)
        probs[b] = (jnp.concatenate(p_parts, axis=0), inv_l)

    def attn_values(b):
        p_all, inv_l = probs.pop(b)
        o = jnp.dot(p_all, vwin[b], preferred_element_type=F32)
        outs = []
        for r in range(REP):
            o0 = o[r * CHUNK:(r + 1) * CHUNK] * inv_l[r]
            o1 = o[(REP + r) * CHUNK:(REP + r + 1) * CHUNK] * inv_l[REP + r]
            outs.append(jnp.where(head_lanes[0], o0, o1))
        attn_scr[b * CHUNK:(b + 1) * CHUNK, :] = jnp.concatenate(outs, axis=1)

    def s5_in(h):
        uh = jnp.concatenate([utb_scr[2 * h], utb_scr[2 * h + 1]], axis=1).astype(BF16)
        bu_scr[:, h * HALF_STATE:(h + 1) * HALF_STATE] = jnp.dot(
            uh, bbd_ref[h], preferred_element_type=F32)

    def s5_scan(h):
        for w in range(HALF_RI // SCAN_W):
            cr = h * HALF_STATE + w * SCAN_W
            ci = cr + HALF_RI
            ar = jnp.broadcast_to(a_ref[2 * h:2 * h + 1, w * SCAN_W:(w + 1) * SCAN_W], (BG, SCAN_W))
            ai = jnp.broadcast_to(a_ref[2 * h + 1:2 * h + 2, w * SCAN_W:(w + 1) * SCAN_W], (BG, SCAN_W))

            def scan_body(t, carry, cr=cr, ci=ci, ar=ar, ai=ai):
                xr, xi = carry
                r = pl.multiple_of(t * BG, BG)
                rl = pl.multiple_of(row0 + t * BG, BG)
                br = bu_scr[pl.ds(rl, BG), cr:cr + SCAN_W]
                bi = bu_scr[pl.ds(rl, BG), ci:ci + SCAN_W]
                nr = ar * xr - ai * xi + br
                ni = ar * xi + ai * xr + bi
                bu_scr[pl.ds(r, BG), cr:cr + SCAN_W] = nr
                bu_scr[pl.ds(r, BG), ci:ci + SCAN_W] = ni
                return nr, ni

            xr0 = st_ref[:, cr:cr + SCAN_W]
            xi0 = st_ref[:, ci:ci + SCAN_W]
            xr, xi = lax.fori_loop(0, CHUNK, scan_body, (xr0, xi0), unroll=True)
            st_ref[:, cr:cr + SCAN_W] = jnp.where(live, xr, xr0)
            st_ref[:, ci:ci + SCAN_W] = jnp.where(live, xi, xi0)

    def s5_out(h):
        xs = bu_scr[pl.ds(pl.multiple_of(row0, ROWS), ROWS), h * HALF_STATE:(h + 1) * HALF_STATE]
        yh = jnp.dot(xs.astype(BF16), ccat_ref[h],
                     preferred_element_type=F32)
        for jj in range(2):
            ytb_scr[2 * h + jj] = yh[:, jj * LANES:(jj + 1) * LANES]

    def glu_in():
        for b in range(BG):
            for j in range(U_SLABS):
                ybt_scr[b * CHUNK:(b + 1) * CHUNK, j * LANES:(j + 1) * LANES] = (
                    ytb_scr[j, pl.ds(b, CHUNK, stride=BG), :])
        ybt_scr[...] = jax.nn.gelu(ybt_scr[...] + du_scr[...])

    def glu_out():
        zs = ybt_scr[...]
        gate = jnp.dot(zs.astype(BF16), wglu_ref[...], preferred_element_type=F32) + bglu_ref[...]
        s_out = zs * jax.nn.sigmoid(gate)
        merged_scr[:, ATTN_W:] = _rms(s_out, gs_ref[...]).astype(BF16)

    def attn_norm():
        merged_scr[:, :ATTN_W] = _rms(attn_scr[...], ga_ref[...]).astype(BF16)

    pieces = {"unpack": unpack, "norm": proj_norm, "W": proj, "QKN": proj_qknorm, "OP": out_proj,
              "S": attn_scores, "PV": attn_values, "BU": s5_in, "SCAN": s5_scan, "C": s5_out,
              "GLU_IN": glu_in, "GLU_OUT": glu_out, "AN": attn_norm}
    for item in _STEP_ORDER.split():
        name, _, arg = item.partition(":")
        pieces[name](*([int(arg)] if arg else []))


def _mlp_kernel(x_ref, d_ref, g2_ref, wup_ref, wdown_ref, o_ref):
    for r in range(0, MLP_ROWS, MLP_SUB_ROWS):
        rows = slice(r, r + MLP_SUB_ROWS)
        h = x_ref[rows, :] + d_ref[rows, :]
        xn = _rms(h, g2_ref[...]).astype(BF16)
        acc = h
        for j in range(D_FF // FF_BLK):
            f = jnp.dot(xn, wup_ref[:, j * FF_BLK:(j + 1) * FF_BLK], preferred_element_type=F32)
            f = jnp.maximum(f, 0.0)
            acc = acc + jnp.dot((f * f).astype(BF16), wdown_ref[j * FF_BLK:(j + 1) * FF_BLK, :],
                                preferred_element_type=F32)
        o_ref[rows, :] = acc


def _mix_chunk(s, n_steps):
    return jnp.clip(s - 1, 0, n_steps - 1)


def _const_spec(shape):
    n = len(shape)
    return pl.BlockSpec(shape, lambda *_: (0,) * n, pipeline_mode=pl.Buffered(1))


def _vmem_limit(blocks, scratch=()):
    need = COMPILER_VMEM_BYTES
    need += sum(n_buf * math.prod(shape) * jnp.dtype(dtype).itemsize for shape, dtype, n_buf in blocks)
    need += sum(math.prod(m.shape) * jnp.dtype(m.dtype).itemsize for m in scratch)
    assert need <= V7X_VMEM_BYTES, need
    return need


def _spec_blocks(specs, arrays):
    return [(spec.block_shape, a.dtype, spec.pipeline_mode.buffer_count if spec.pipeline_mode else 2)
            for spec, a in zip(specs, arrays) if spec.block_shape is not None]


def _meta_call(meta, w):
    args = (meta, w["g1"], w["w_in"], w["e"], w["gqk"], w["a"], w["bbd"])
    out_shape = (jax.ShapeDtypeStruct((N_META, KV_W), F32),
                 jax.ShapeDtypeStruct((N_META, KV_W), F32),
                 jax.ShapeDtypeStruct((1, STATE_W), F32))
    scratch = [pltpu.VMEM((N_META, STATE_W), F32)]
    whole = [(a.shape, a.dtype, 1) for a in args + out_shape]
    return pl.pallas_call(
        _meta_kernel, out_shape=out_shape, scratch_shapes=scratch,
        compiler_params=pltpu.CompilerParams(vmem_limit_bytes=_vmem_limit(whole, scratch)),
        name="meta",
    )(*args)


def _mixer_call(x, st0, k0, v0, w, *, chunk_offset, n_out):
    nb, s_len, _ = x.shape
    n_chunks = s_len // CHUNK
    n_bg = nb // BG
    n_steps = n_bg * n_chunks
    assert n_chunks % n_out == 0
    grid = (n_steps + 2,)

    def chunk_block(gidx):
        return (gidx // n_chunks, gidx % n_chunks, 0)

    x_index = lambda s: chunk_block(jnp.minimum(s, n_steps - 1))
    d_index = lambda s: chunk_block(jnp.clip(s - 2, 0, n_steps - 1))
    per_bg = lambda s: (_mix_chunk(s, n_steps) // n_chunks, 0, 0)
    per_bg2 = lambda s: (_mix_chunk(s, n_steps) // n_chunks, 0)
    in_specs = [
        pl.BlockSpec((BG, CHUNK, D_MODEL), x_index),
        pl.BlockSpec((BG, STATE_W), per_bg2),
        pl.BlockSpec((BG, N_META + WINDOW, KV_W), per_bg),
        pl.BlockSpec((BG, N_META + WINDOW, KV_W), per_bg),
        _const_spec((1, D_MODEL)),
        _const_spec((D_MODEL, IN_W)),
        _const_spec((E_TILE, E_TILE)),
        _const_spec((1, QK_W)),
        _const_spec((N_Q_HEADS, KEY_ROWS)),
        pl.BlockSpec(memory_space=pltpu.SMEM),
        _const_spec((4, HALF_RI)),
        _const_spec((2, HALF_U, HALF_STATE)),
        _const_spec((2, HALF_STATE, HALF_U)),
        _const_spec((1, SSM_W)),
        _const_spec((SSM_W, SSM_W)),
        _const_spec((1, SSM_W)),
        _const_spec((1, ATTN_W)),
        _const_spec((1, SSM_W)),
        _const_spec((D_MODEL, D_MODEL)),
    ]
    out_specs = (
        pl.BlockSpec((BG, CHUNK, D_MODEL), d_index),
        pl.BlockSpec((BG, n_out * CHUNK, KV_W), per_bg),
        pl.BlockSpec((BG, n_out * CHUNK, KV_W), per_bg),
        pl.BlockSpec((BG, STATE_W), per_bg2),
    )
    out_shape = (
        jax.ShapeDtypeStruct((nb, s_len, D_MODEL), F32),
        jax.ShapeDtypeStruct((nb, n_out * CHUNK, KV_W), F32),
        jax.ShapeDtypeStruct((nb, n_out * CHUNK, KV_W), F32),
        jax.ShapeDtypeStruct((nb, STATE_W), F32),
    )
    scratch = [
        pltpu.VMEM((ROWS, IN_W), F32),
        pltpu.VMEM((ROWS, D_MODEL), BF16),
        pltpu.VMEM((BG, N_Q_HEADS * CHUNK, LANES), BF16),
        pltpu.VMEM((ROWS, SSM_W), F32),
        pltpu.VMEM((ROWS, ATTN_W), F32),
        pltpu.VMEM((U_SLABS, ROWS, LANES), F32),
        pltpu.VMEM((ROWS, STATE_W), F32),
        pltpu.VMEM((U_SLABS, ROWS, LANES), F32),
        pltpu.VMEM((ROWS, SSM_W), F32),
        pltpu.VMEM((BG, KEY_ROWS, KV_W), BF16),
        pltpu.VMEM((BG, KEY_ROWS, KV_W), BF16),
        pltpu.VMEM((ROWS, D_MODEL), BF16),
    ]
    kern = functools.partial(_mixer_kernel, n_chunks=n_chunks, n_bg=n_bg,
                             chunk_offset=chunk_offset, n_out=n_out)
    args = (x, st0, k0, v0,
            w["g1"], w["w_in"], w["e"], w["gqk"], w["fill"], jnp.zeros((1,), jnp.int32),
            w["a"], w["bbd"], w["ccat"], w["dvec"],
            w["w_glu"], w["b_glu"], w["ga"], w["gs"], w["w_out"])
    blocks = _spec_blocks(in_specs, args) + _spec_blocks(out_specs, out_shape)
    return pl.pallas_call(
        kern, grid=grid, in_specs=in_specs, out_specs=out_specs, out_shape=out_shape,
        scratch_shapes=scratch,
        compiler_params=pltpu.CompilerParams(
            dimension_semantics=("arbitrary",), vmem_limit_bytes=_vmem_limit(blocks, scratch)),
        name="mixer",
    )(*args)


def _mlp_call(x, d, w):
    n = x.shape[0]
    rows = pl.BlockSpec((MLP_ROWS, D_MODEL), lambda i: (i, 0))
    in_specs = [rows, rows, _const_spec((1, D_MODEL)), _const_spec((D_MODEL, D_FF)),
                _const_spec((D_FF, D_MODEL))]
    args = (x, d, w["g2"], w["w_up"], w["w_down"])
    out_shape = jax.ShapeDtypeStruct((n, D_MODEL), F32)
    blocks = _spec_blocks(in_specs, args) + _spec_blocks([rows], [out_shape])
    return pl.pallas_call(
        _mlp_kernel, grid=(n // MLP_ROWS,), in_specs=in_specs, out_specs=rows, out_shape=out_shape,
        compiler_params=pltpu.CompilerParams(
            dimension_semantics=("arbitrary",), vmem_limit_bytes=_vmem_limit(blocks)),
        name="mlp",
    )(*args)


def _prep_weights(norm1_g, w_in, q_norm_g, k_norm_g, sinks, ssm_A_re, ssm_A_im, ssm_log_dt,
                  ssm_B_re, ssm_B_im, ssm_C_re, ssm_C_im, ssm_D, w_glu, b_glu, attn_out_g,
                  ssm_out_g, w_out, norm2_g, w_up, w_down):
    def attn_perm(m):
        lead = m.shape[:-1]
        return m.reshape(lead + (N_KV_HEADS, REP, HEAD_DIM)).swapaxes(-3, -2).reshape(lead + (ATTN_W,))

    w_in_p = jnp.concatenate([attn_perm(w_in[:, :ATTN_W]), w_in[:, ATTN_W:]], axis=1).astype(BF16)
    e = jnp.kron(jnp.eye(E_TILE // HEAD_DIM, dtype=F32),
                 jnp.full((HEAD_DIM, HEAD_DIM), 1.0 / HEAD_DIM, F32)).astype(BF16)
    gqk = jnp.concatenate([jnp.tile(q_norm_g, N_Q_HEADS) * (HEAD_DIM ** -0.5 * LOG2E),
                           jnp.tile(k_norm_g, N_KV_HEADS)])[None, :]

    fill = jnp.full((N_Q_HEADS, KEY_ROWS), NEG_INF, F32).at[:, N_KEYS].set(sinks * LOG2E)

    lam_re = jnp.minimum(ssm_A_re, EIG_CLIP)
    lam_im = ssm_A_im
    dt = jnp.exp(ssm_log_dt)[:, None]
    mag = jnp.exp(lam_re * dt)
    ar = mag * jnp.cos(lam_im * dt)
    ai = mag * jnp.sin(lam_im * dt)
    den = lam_re * lam_re + lam_im * lam_im
    cr = ((ar - 1.0) * lam_re + ai * lam_im) / den
    ci = (ai * lam_re - (ar - 1.0) * lam_im) / den
    bb_re = cr[..., None] * ssm_B_re - ci[..., None] * ssm_B_im
    bb_im = cr[..., None] * ssm_B_im + ci[..., None] * ssm_B_re
    a = jnp.stack([ar.reshape(2, HALF_RI), ai.reshape(2, HALF_RI)], axis=1).reshape(4, HALF_RI)
    eye = jnp.eye(HALF_G, dtype=F32)
    bb = jnp.stack([bb_re, bb_im], 0).reshape(2, 2, HALF_G, SSM_P, SSM_CH)
    bbd = jnp.einsum('ab,rhapc->hacrbp', eye, bb).reshape(2, HALF_U, HALF_STATE).astype(BF16)
    cc = jnp.stack([ssm_C_re, -ssm_C_im], 0).reshape(2, 2, HALF_G, SSM_CH, SSM_P)
    ccat = jnp.einsum('ab,rhacp->hrapbc', eye, cc).reshape(2, HALF_STATE, HALF_U).astype(BF16)

    w_out_p = jnp.concatenate(
        [attn_perm(w_out[:ATTN_W].T).T, w_out[ATTN_W:]], axis=0).astype(BF16)
    return {
        "g1": norm1_g[None, :], "w_in": w_in_p, "e": e, "gqk": gqk, "fill": fill,
        "a": a, "bbd": bbd, "ccat": ccat, "dvec": ssm_D.reshape(1, SSM_W),
        "w_glu": w_glu.astype(BF16), "b_glu": b_glu[None, :],
        "ga": attn_perm(attn_out_g)[None, :], "gs": ssm_out_g[None, :], "w_out": w_out_p,
        "g2": norm2_g[None, :], "w_up": w_up.astype(BF16), "w_down": w_down.astype(BF16),
    }


def _state_to_lanes(re, im):
    nb = re.shape[0]
    return jnp.stack([re.reshape(nb, 2, HALF_RI), im.reshape(nb, 2, HALF_RI)], axis=2).reshape(nb, STATE_W)


def _lanes_to_state(st):
    nb = st.shape[0]
    s = st.reshape(nb, 2, 2, HALF_RI)
    return s[:, :, 0].reshape(nb, SSM_G, SSM_P), s[:, :, 1].reshape(nb, SSM_G, SSM_P)


def kernel(x_prompt, x_sample, cache_swa_k, cache_swa_v, state_ssm_re, state_ssm_im, meta_tokens, norm1_g, w_in, q_norm_g, k_norm_g, sinks, ssm_A_re, ssm_A_im, ssm_log_dt, ssm_B_re, ssm_B_im, ssm_C_re, ssm_C_im, ssm_D, w_glu, b_glu, attn_out_g, ssm_out_g, w_out, norm2_g, w_up, w_down):
    assert norm1_g.shape[0] == 1, "single-layer trunk"
    nb, seq, _ = x_prompt.shape
    nd, dseq, _ = x_sample.shape
    assert dseq == CHUNK and seq % CHUNK == 0 and seq >= WINDOW and nb % BG == 0 and nd % BG == 0
    w = _prep_weights(norm1_g[0], w_in[0], q_norm_g[0], k_norm_g[0], sinks[0], ssm_A_re[0],
                      ssm_A_im[0], ssm_log_dt[0], ssm_B_re[0], ssm_B_im[0], ssm_C_re[0],
                      ssm_C_im[0], ssm_D[0], w_glu[0], b_glu[0], attn_out_g[0], ssm_out_g[0],
                      w_out[0], norm2_g[0], w_up[0], w_down[0])

    k_meta, v_meta, st_meta = _meta_call(meta_tokens, w)
    kmeta_b = jnp.broadcast_to(k_meta[None], (nb, N_META, KV_W))
    vmeta_b = jnp.broadcast_to(v_meta[None], (nb, N_META, KV_W))
    empty_win = ((0, 0), (0, WINDOW), (0, 0))
    dp, kp, vp, stp = _mixer_call(
        x_prompt, jnp.broadcast_to(st_meta, (nb, STATE_W)), jnp.pad(kmeta_b, empty_win),
        jnp.pad(vmeta_b, empty_win), w, chunk_offset=0, n_out=WINDOW // CHUNK)
    y_prompt = _mlp_call(x_prompt.reshape(nb * seq, D_MODEL), dp.reshape(nb * seq, D_MODEL),
                         w).reshape(nb, seq, D_MODEL)

    ck = cache_swa_k[0].reshape(nd, N_META + WINDOW, KV_W)
    cv = cache_swa_v[0].reshape(nd, N_META + WINDOW, KV_W)
    ds, ks, vs, sts = _mixer_call(
        x_sample, _state_to_lanes(state_ssm_re[0], state_ssm_im[0]),
        ck, cv, w, chunk_offset=WINDOW // CHUNK, n_out=1)
    y_sample = _mlp_call(x_sample.reshape(nd * dseq, D_MODEL), ds.reshape(nd * dseq, D_MODEL),
                         w).reshape(nd, dseq, D_MODEL)

    kv5 = lambda t: t.reshape(t.shape[0], t.shape[1], N_KV_HEADS, HEAD_DIM)[None]
    new_k_prompt = kv5(jnp.concatenate([kmeta_b, kp], axis=1))
    new_v_prompt = kv5(jnp.concatenate([vmeta_b, vp], axis=1))
    srp, sip = _lanes_to_state(stp)
    srs, sis = _lanes_to_state(sts)
    return (y_prompt, y_sample, new_k_prompt, new_v_prompt, srp[None], sip[None],
            kv5(ks), kv5(vs), srs[None], sis[None])
```

```python
import functools
import math

import jax
import jax.numpy as jnp
from jax import lax
from jax.experimental import pallas as pl
from jax.experimental.pallas import tpu as pltpu

D_MODEL = 1024
CHUNK = 64
N_META = 16
HEAD_DIM = 64
ATTN_W = 512
N_KV_HEADS = 2
REP = 4
N_Q_HEADS = N_KV_HEADS * REP
KV_W = N_KV_HEADS * HEAD_DIM
WINDOW = 128
SSM_W = 512
SSM_CH = 16
SSM_G = 32
SSM_P = 64
D_FF = 4 * D_MODEL
IN_W = ATTN_W + 2 * KV_W + SSM_W
EPS = 1e-6
EIG_CLIP = -1e-4
NEG_INF = -1e30

SUBLANES = 8
LANES = 128
V7X_VMEM_BYTES = 64 * 1024 * 1024
COMPILER_VMEM_BYTES = 12 * 1024 * 1024

BG = SUBLANES
ROWS = BG * CHUNK
QK_W = ATTN_W + KV_W
V_OFF = QK_W
U_OFF = QK_W + KV_W
N_KEYS = N_META + WINDOW + CHUNK
KEY_ROWS = 256
LOG2E = 1.4426950408889634
N_SLOTS = WINDOW // CHUNK + 1
HALF_G = SSM_G // 2
HALF_RI = HALF_G * SSM_P
HALF_STATE = 2 * HALF_RI
STATE_W = 2 * HALF_STATE
HALF_U = HALF_G * SSM_CH
U_SLABS = SSM_W // LANES
SCAN_W = 512
MLP_ROWS = 1024
MLP_SUB_ROWS = 256
E_TILE = 256
MXU_N = 256
FF_BLK = 1024

BF16 = jnp.bfloat16
F32 = jnp.float32

_STEP_ORDER = (
    "unpack norm GLU_OUT AN BU:0 S:0 BU:1 SCAN:0 S:1 OP:0 PV:0 S:2 OP:1 SCAN:1 PV:1 S:3 OP:2 PV:2 S:4 OP:3 "
    "PV:3 S:5 W:0 PV:4 S:6 W:1 PV:5 QKN:0 S:7 W:2 W:3 QKN:1 PV:6 C:0 QKN:2 PV:7 C:1 W:4 GLU_IN")


def _rms(x, g):
    ms = jnp.mean(x * x, axis=-1, keepdims=True)
    return x * lax.rsqrt(ms + EPS) * g


def _project(x, g1_ref, win_ref, e_ref, gqk_ref):
    xn = _rms(x, g1_ref[...]).astype(BF16)
    z = jnp.dot(xn, win_ref[...], preferred_element_type=F32)
    zqk = z[:, :QK_W]
    sq = (zqk * zqk).astype(BF16)
    ms = jnp.concatenate(
        [jnp.dot(sq[:, o:o + n], e_ref[:n, :n], preferred_element_type=F32)
         for o, n in ((0, E_TILE), (E_TILE, E_TILE), (2 * E_TILE, QK_W - 2 * E_TILE))], axis=1)
    qk = zqk * lax.rsqrt(ms + EPS) * gqk_ref[...]
    return qk, z


def _meta_kernel(meta_ref, g1_ref, win_ref, e_ref, gqk_ref, a_ref, bbd_ref,
                 k_ref, v_ref, st_ref, bu_scr):
    qk, z = _project(meta_ref[...], g1_ref, win_ref, e_ref, gqk_ref)
    k_ref[...] = qk[:, ATTN_W:]
    v_ref[...] = z[:, V_OFF:U_OFF]
    u = z[:, U_OFF:].astype(BF16)
    for h in range(2):
        bu_scr[:, h * HALF_STATE:(h + 1) * HALF_STATE] = jnp.dot(
            u[:, h * HALF_U:(h + 1) * HALF_U], bbd_ref[h], preferred_element_type=F32)
    for h in range(2):
        ar = a_ref[2 * h:2 * h + 1, :]
        ai = a_ref[2 * h + 1:2 * h + 2, :]
        cr = h * HALF_STATE
        ci = cr + HALF_RI
        xr = jnp.zeros((1, HALF_RI), F32)
        xi = jnp.zeros((1, HALF_RI), F32)
        for t in range(N_META):
            br = bu_scr[t:t + 1, cr:cr + HALF_RI]
            bi = bu_scr[t:t + 1, ci:ci + HALF_RI]
            xr, xi = ar * xr - ai * xi + br, ar * xi + ai * xr + bi
        st_ref[:, cr:cr + HALF_RI] = xr
        st_ref[:, ci:ci + HALF_RI] = xi


def _mixer_kernel(x_ref, st0_ref, k0_ref, v0_ref,
                  g1_ref, win_ref, e_ref, gqk_ref, fill_ref, row0_ref, a_ref, bbd_ref, ccat_ref, dvec_ref,
                  wglu_ref, bglu_ref, ga_ref, gs_ref, wout_ref,
                  d_ref, kout_ref, vout_ref, st_ref,
                  z_scr, xn_scr, qs_scr, du_scr, attn_scr, utb_scr, bu_scr, ytb_scr, ybt_scr, kwin, vwin,
                  merged_scr, *, n_chunks, n_bg, chunk_offset, n_out):
    s = pl.program_id(0)
    n_steps = n_bg * n_chunks
    c = _mix_chunk(s, n_steps) % n_chunks
    live = jnp.logical_and(s >= 1, s <= n_steps)
    gc = c + chunk_offset
    row0 = row0_ref[0]

    @pl.when(s == 0)
    def _():
        z_scr[...] = jnp.zeros((ROWS, IN_W), F32)
        ybt_scr[...] = jnp.zeros((ROWS, SSM_W), F32)
        attn_scr[...] = jnp.zeros((ROWS, ATTN_W), F32)

    @pl.when(jnp.logical_and(c == 0, s <= n_steps))
    def _():
        st_ref[...] = st0_ref[...]
        kwin[:, 0:N_META + WINDOW, :] = k0_ref[...].astype(BF16)
        vwin[:, 0:N_META + WINDOW, :] = v0_ref[...].astype(BF16)
        kwin[:, N_META + WINDOW:, :] = jnp.zeros((BG, KEY_ROWS - N_META - WINDOW, KV_W), BF16)
        vwin[:, N_META + WINDOW:, :] = jnp.zeros((BG, KEY_ROWS - N_META - WINDOW, KV_W), BF16)

    lane = lax.broadcasted_iota(jnp.int32, (CHUNK, LANES), 1)
    head_lanes = (lane < HEAD_DIM, lane >= HEAD_DIM)
    key_limit = jnp.where(gc >= N_SLOTS - 1, N_KEYS, N_META + CHUNK * (gc + 1))
    key_ok = lax.broadcasted_iota(jnp.int32, (1, KEY_ROWS), 1) < key_limit

    def unpack():
        k = z_scr[:, ATTN_W:QK_W].reshape(BG, CHUNK, KV_W)
        v = z_scr[:, V_OFF:U_OFF].reshape(BG, CHUNK, KV_W)
        u = z_scr[:, U_OFF:]
        slot_row = pl.multiple_of(N_META + lax.rem(gc, N_SLOTS) * CHUNK, 16)
        kwin[:, pl.ds(slot_row, CHUNK), :] = k.astype(BF16)
        vwin[:, pl.ds(slot_row, CHUNK), :] = v.astype(BF16)
        out_row = pl.multiple_of(lax.rem(c, n_out) * CHUNK, CHUNK)
        kout_ref[:, pl.ds(out_row, CHUNK), :] = k
        vout_ref[:, pl.ds(out_row, CHUNK), :] = v
        for b in range(BG):
            qb = z_scr[b * CHUNK:(b + 1) * CHUNK, 0:ATTN_W]
            parts = [jnp.where(head_lanes[kvh], qb[:, r * LANES:(r + 1) * LANES], 0.0)
                     for kvh in range(N_KV_HEADS) for r in range(REP)]
            qs_scr[b] = jnp.concatenate(parts, axis=0).astype(BF16)
        for b in range(BG):
            for j in range(U_SLABS):
                utb_scr[j, pl.ds(b, CHUNK, stride=BG), :] = (
                    u[b * CHUNK:(b + 1) * CHUNK, j * LANES:(j + 1) * LANES])
        du_scr[...] = dvec_ref[...] * u

    def out_proj(j):
        cols = slice(j * MXU_N, (j + 1) * MXU_N)
        d_ref[:, :, cols] = jnp.dot(merged_scr[...], wout_ref[:, cols],
                                    preferred_element_type=F32).reshape(BG, CHUNK, MXU_N)

    def proj_norm():
        xn_scr[...] = _rms(x_ref[...].reshape(ROWS, D_MODEL), g1_ref[...]).astype(BF16)

    def proj(j):
        cols = slice(j * MXU_N, (j + 1) * MXU_N)
        z_scr[:, cols] = jnp.dot(xn_scr[...], win_ref[:, cols], preferred_element_type=F32)

    def proj_qknorm(j):
        o = j * E_TILE
        n = min(E_TILE, QK_W - o)
        zq = z_scr[:, o:o + n]
        ms = jnp.dot((zq * zq).astype(BF16), e_ref[:n, :n], preferred_element_type=F32)
        z_scr[:, o:o + n] = zq * lax.rsqrt(ms + EPS) * gqk_ref[:, o:o + n]

    probs = {}

    def attn_scores(b):
        sc = lax.dot_general(qs_scr[b], kwin[b], (((1,), (1,)), ((), ())), preferred_element_type=F32)
        p_parts = []
        inv_l = []
        for i in range(N_Q_HEADS):
            sb = jnp.where(key_ok, sc[i * CHUNK:(i + 1) * CHUNK], fill_ref[i:i + 1, :])
            m = jnp.max(sb, axis=-1, keepdims=True)
            p = jnp.exp2(sb - m)
            l = jnp.sum(p, axis=-1, keepdims=True)
            p_parts.append(p.astype(BF16))
            inv_l.append(1.0 / l)
        probs[b] = (jnp.concatenate(p_parts, axis=0), inv_l)

    def attn_values(b):
        p_all, inv_l = probs.pop(b)
        o = jnp.dot(p_all, vwin[b], preferred_element_type=F32)
        outs = []
        for r in range(REP):
            o0 = o[r * CHUNK:(r + 1) * CHUNK] * inv_l[r]
            o1 = o[(REP + r) * CHUNK:(REP + r + 1) * CHUNK] * inv_l[REP + r]
            outs.append(jnp.where(head_lanes[0], o0, o1))
        attn_scr[b * CHUNK:(b + 1) * CHUNK, :] = jnp.concatenate(outs, axis=1)

    def s5_in(h):
        uh = jnp.concatenate([utb_scr[2 * h], utb_scr[2 * h + 1]], axis=1).astype(BF16)
        bu_scr[:, h * HALF_STATE:(h + 1) * HALF_STATE] = jnp.dot(
            uh, bbd_ref[h], preferred_element_type=F32)

    def s5_scan(h):
        for w in range(HALF_RI // SCAN_W):
            cr = h * HALF_STATE + w * SCAN_W
            ci = cr + HALF_RI
            ar = jnp.broadcast_to(a_ref[2 * h:2 * h + 1, w * SCAN_W:(w + 1) * SCAN_W], (BG, SCAN_W))
            ai = jnp.broadcast_to(a_ref[2 * h + 1:2 * h + 2, w * SCAN_W:(w + 1) * SCAN_W], (BG, SCAN_W))

            def scan_body(t, carry, cr=cr, ci=ci, ar=ar, ai=ai):
                xr, xi = carry
                r = pl.multiple_of(t * BG, BG)
                rl = pl.multiple_of(row0 + t * BG, BG)
                br = bu_scr[pl.ds(rl, BG), cr:cr + SCAN_W]
                bi = bu_scr[pl.ds(rl, BG), ci:ci + SCAN_W]
                nr = ar * xr - ai * xi + br
                ni = ar * xi + ai * xr + bi
                bu_scr[pl.ds(r, BG), cr:cr + SCAN_W] = nr
                bu_scr[pl.ds(r, BG), ci:ci + SCAN_W] = ni
                return nr, ni

            xr0 = st_ref[:, cr:cr + SCAN_W]
            xi0 = st_ref[:, ci:ci + SCAN_W]
            xr, xi = lax.fori_loop(0, CHUNK, scan_body, (xr0, xi0), unroll=True)
            st_ref[:, cr:cr + SCAN_W] = jnp.where(live, xr, xr0)
            st_ref[:, ci:ci + SCAN_W] = jnp.where(live, xi, xi0)

    def s5_out(h):
        xs = bu_scr[pl.ds(pl.multiple_of(row0, ROWS), ROWS), h * HALF_STATE:(h + 1) * HALF_STATE]
        yh = jnp.dot(xs.astype(BF16), ccat_ref[h],
                     preferred_element_type=F32)
        for jj in range(2):
            ytb_scr[2 * h + jj] = yh[:, jj * LANES:(jj + 1) * LANES]

    def glu_in():
        for b in range(BG):
            for j in range(U_SLABS):
                ybt_scr[b * CHUNK:(b + 1) * CHUNK, j * LANES:(j + 1) * LANES] = (
                    ytb_scr[j, pl.ds(b, CHUNK, stride=BG), :])
        ybt_scr[...] = jax.nn.gelu(ybt_scr[...] + du_scr[...])

    def glu_out():
        zs = ybt_scr[...]
        gate = jnp.dot(zs.astype(BF16), wglu_ref[...], preferred_element_type=F32) + bglu_ref[...]
        s_out = zs * jax.nn.sigmoid(gate)
        merged_scr[:, ATTN_W:] = _rms(s_out, gs_ref[...]).astype(BF16)

    def attn_norm():
        merged_scr[:, :ATTN_W] = _rms(attn_scr[...], ga_ref[...]).astype(BF16)

    pieces = {"unpack": unpack, "norm": proj_norm, "W": proj, "QKN": proj_qknorm, "OP": out_proj,
              "S": attn_scores, "PV": attn_values, "BU": s5_in, "SCAN": s5_scan, "C": s5_out,
              "GLU_IN": glu_in, "GLU_OUT": glu_out, "AN": attn_norm}
    for item in _STEP_ORDER.split():
        name, _, arg = item.partition(":")
        pieces[name](*([int(arg)] if arg else []))


def _mlp_kernel(x_ref, d_ref, g2_ref, wup_ref, wdown_ref, o_ref):
    for r in range(0, MLP_ROWS, MLP_SUB_ROWS):
        rows = slice(r, r + MLP_SUB_ROWS)
        h = x_ref[rows, :] + d_ref[rows, :]
        xn = _rms(h, g2_ref[...]).astype(BF16)
        acc = h
        for j in range(D_FF // FF_BLK):
            f = jnp.dot(xn, wup_ref[:, j * FF_BLK:(j + 1) * FF_BLK], preferred_element_type=F32)
            f = jnp.maximum(f, 0.0)
            acc = acc + jnp.dot((f * f).astype(BF16), wdown_ref[j * FF_BLK:(j + 1) * FF_BLK, :],
                                preferred_element_type=F32)
        o_ref[rows, :] = acc


def _mix_chunk(s, n_steps):
    return jnp.clip(s - 1, 0, n_steps - 1)


def _const_spec(shape):
    n = len(shape)
    return pl.BlockSpec(shape, lambda *_: (0,) * n, pipeline_mode=pl.Buffered(1))


def _vmem_limit(blocks, scratch=()):
    need = COMPILER_VMEM_BYTES
    need += sum(n_buf * math.prod(shape) * jnp.dtype(dtype).itemsize for shape, dtype, n_buf in blocks)
    need += sum(math.prod(m.shape) * jnp.dtype(m.dtype).itemsize for m in scratch)
    assert need <= V7X_VMEM_BYTES, need
    return need


def _spec_blocks(specs, arrays):
    return [(spec.block_shape, a.dtype, spec.pipeline_mode.buffer_count if spec.pipeline_mode else 2)
            for spec, a in zip(specs, arrays) if spec.block_shape is not None]


def _meta_call(meta, w):
    args = (meta, w["g1"], w["w_in"], w["e"], w["gqk"], w["a"], w["bbd"])
    out_shape = (jax.ShapeDtypeStruct((N_META, KV_W), F32),
                 jax.ShapeDtypeStruct((N_META, KV_W), F32),
                 jax.ShapeDtypeStruct((1, STATE_W), F32))
    scratch = [pltpu.VMEM((N_META, STATE_W), F32)]
    whole = [(a.shape, a.dtype, 1) for a in args + out_shape]
    return pl.pallas_call(
        _meta_kernel, out_shape=out_shape, scratch_shapes=scratch,
        compiler_params=pltpu.CompilerParams(vmem_limit_bytes=_vmem_limit(whole, scratch)),
        name="meta",
    )(*args)


def _mixer_call(x, st0, k0, v0, w, *, chunk_offset, n_out):
    nb, s_len, _ = x.shape
    n_chunks = s_len // CHUNK
    n_bg = nb // BG
    n_steps = n_bg * n_chunks
    assert n_chunks % n_out == 0
    grid = (n_steps + 2,)

    def chunk_block(gidx):
        return (gidx // n_chunks, gidx % n_chunks, 0)

    x_index = lambda s: chunk_block(jnp.minimum(s, n_steps - 1))
    d_index = lambda s: chunk_block(jnp.clip(s - 2, 0, n_steps - 1))
    per_bg = lambda s: (_mix_chunk(s, n_steps) // n_chunks, 0, 0)
    per_bg2 = lambda s: (_mix_chunk(s, n_steps) // n_chunks, 0)
    in_specs = [
        pl.BlockSpec((BG, CHUNK, D_MODEL), x_index),
        pl.BlockSpec((BG, STATE_W), per_bg2),
        pl.BlockSpec((BG, N_META + WINDOW, KV_W), per_bg),
        pl.BlockSpec((BG, N_META + WINDOW, KV_W), per_bg),
        _const_spec((1, D_MODEL)),
        _const_spec((D_MODEL, IN_W)),
        _const_spec((E_TILE, E_TILE)),
        _const_spec((1, QK_W)),
        _const_spec((N_Q_HEADS, KEY_ROWS)),
        pl.BlockSpec(memory_space=pltpu.SMEM),
        _const_spec((4, HALF_RI)),
        _const_spec((2, HALF_U, HALF_STATE)),
        _const_spec((2, HALF_STATE, HALF_U)),
        _const_spec((1, SSM_W)),
        _const_spec((SSM_W, SSM_W)),
        _const_spec((1, SSM_W)),
        _const_spec((1, ATTN_W)),
        _const_spec((1, SSM_W)),
        _const_spec((D_MODEL, D_MODEL)),
    ]
    out_specs = (
        pl.BlockSpec((BG, CHUNK, D_MODEL), d_index),
        pl.BlockSpec((BG, n_out * CHUNK, KV_W), per_bg),
        pl.BlockSpec((BG, n_out * CHUNK, KV_W), per_bg),
        pl.BlockSpec((BG, STATE_W), per_bg2),
    )
    out_shape = (
        jax.ShapeDtypeStruct((nb, s_len, D_MODEL), F32),
        jax.ShapeDtypeStruct((nb, n_out * CHUNK, KV_W), F32),
        jax.ShapeDtypeStruct((nb, n_out * CHUNK, KV_W), F32),
        jax.ShapeDtypeStruct((nb, STATE_W), F32),
    )
    scratch = [
        pltpu.VMEM((ROWS, IN_W), F32),
        pltpu.VMEM((ROWS, D_MODEL), BF16),
        pltpu.VMEM((BG, N_Q_HEADS * CHUNK, LANES), BF16),
        pltpu.VMEM((ROWS, SSM_W), F32),
        pltpu.VMEM((ROWS, ATTN_W), F32),
        pltpu.VMEM((U_SLABS, ROWS, LANES), F32),
        pltpu.VMEM((ROWS, STATE_W), F32),
        pltpu.VMEM((U_SLABS, ROWS, LANES), F32),
        pltpu.VMEM((ROWS, SSM_W), F32),
        pltpu.VMEM((BG, KEY_ROWS, KV_W), BF16),
        pltpu.VMEM((BG, KEY_ROWS, KV_W), BF16),
        pltpu.VMEM((ROWS, D_MODEL), BF16),
    ]
    kern = functools.partial(_mixer_kernel, n_chunks=n_chunks, n_bg=n_bg,
                             chunk_offset=chunk_offset, n_out=n_out)
    args = (x, st0, k0, v0,
            w["g1"], w["w_in"], w["e"], w["gqk"], w["fill"], jnp.zeros((1,), jnp.int32),
            w["a"], w["bbd"], w["ccat"], w["dvec"],
            w["w_glu"], w["b_glu"], w["ga"], w["gs"], w["w_out"])
    blocks = _spec_blocks(in_specs, args) + _spec_blocks(out_specs, out_shape)
    return pl.pallas_call(
        kern, grid=grid, in_specs=in_specs, out_specs=out_specs, out_shape=out_shape,
        scratch_shapes=scratch,
        compiler_params=pltpu.CompilerParams(
            dimension_semantics=("arbitrary",), vmem_limit_bytes=_vmem_limit(blocks, scratch)),
        name="mixer",
    )(*args)


def _mlp_call(x, d, w):
    n = x.shape[0]
    rows = pl.BlockSpec((MLP_ROWS, D_MODEL), lambda i: (i, 0))
    in_specs = [rows, rows, _const_spec((1, D_MODEL)), _const_spec((D_MODEL, D_FF)),
                _const_spec((D_FF, D_MODEL))]
    args = (x, d, w["g2"], w["w_up"], w["w_down"])
    out_shape = jax.ShapeDtypeStruct((n, D_MODEL), F32)
    blocks = _spec_blocks(in_specs, args) + _spec_blocks([rows], [out_shape])
    return pl.pallas_call(
        _mlp_kernel, grid=(n // MLP_ROWS,), in_specs=in_specs, out_specs=rows, out_shape=out_shape,
        compiler_params=pltpu.CompilerParams(
            dimension_semantics=("arbitrary",), vmem_limit_bytes=_vmem_limit(blocks)),
        name="mlp",
    )(*args)


def _prep_weights(norm1_g, w_in, q_norm_g, k_norm_g, sinks, ssm_A_re, ssm_A_im, ssm_log_dt,
                  ssm_B_re, ssm_B_im, ssm_C_re, ssm_C_im, ssm_D, w_glu, b_glu, attn_out_g,
                  ssm_out_g, w_out, norm2_g, w_up, w_down):
    def attn_perm(m):
        lead = m.shape[:-1]
        return m.reshape(lead + (N_KV_HEADS, REP, HEAD_DIM)).swapaxes(-3, -2).reshape(lead + (ATTN_W,))

    w_in_p = jnp.concatenate([attn_perm(w_in[:, :ATTN_W]), w_in[:, ATTN_W:]], axis=1).astype(BF16)
    e = jnp.kron(jnp.eye(E_TILE // HEAD_DIM, dtype=F32),
                 jnp.full((HEAD_DIM, HEAD_DIM), 1.0 / HEAD_DIM, F32)).astype(BF16)
    gqk = jnp.concatenate([jnp.tile(q_norm_g, N_Q_HEADS) * (HEAD_DIM ** -0.5 * LOG2E),
                           jnp.tile(k_norm_g, N_KV_HEADS)])[None, :]

    fill = jnp.full((N_Q_HEADS, KEY_ROWS), NEG_INF, F32).at[:, N_KEYS].set(sinks * LOG2E)

    lam_re = jnp.minimum(ssm_A_re, EIG_CLIP)
    lam_im = ssm_A_im
    dt = jnp.exp(ssm_log_dt)[:, None]
    mag = jnp.exp(lam_re * dt)
    ar = mag * jnp.cos(lam_im * dt)
    ai = mag * jnp.sin(lam_im * dt)
    den = lam_re * lam_re + lam_im * lam_im
    cr = ((ar - 1.0) * lam_re + ai * lam_im) / den
    ci = (ai * lam_re - (ar - 1.0) * lam_im) / den
    bb_re = cr[..., None] * ssm_B_re - ci[..., None] * ssm_B_im
    bb_im = cr[..., None] * ssm_B_im + ci[..., None] * ssm_B_re
    a = jnp.stack([ar.reshape(2, HALF_RI), ai.reshape(2, HALF_RI)], axis=1).reshape(4, HALF_RI)
    eye = jnp.eye(HALF_G, dtype=F32)
    bb = jnp.stack([bb_re, bb_im], 0).reshape(2, 2, HALF_G, SSM_P, SSM_CH)
    bbd = jnp.einsum('ab,rhapc->hacrbp', eye, bb).reshape(2, HALF_U, HALF_STATE).astype(BF16)
    cc = jnp.stack([ssm_C_re, -ssm_C_im], 0).reshape(2, 2, HALF_G, SSM_CH, SSM_P)
    ccat = jnp.einsum('ab,rhacp->hrapbc', eye, cc).reshape(2, HALF_STATE, HALF_U).astype(BF16)

    w_out_p = jnp.concatenate(
        [attn_perm(w_out[:ATTN_W].T).T, w_out[ATTN_W:]], axis=0).astype(BF16)
    return {
        "g1": norm1_g[None, :], "w_in": w_in_p, "e": e, "gqk": gqk, "fill": fill,
        "a": a, "bbd": bbd, "ccat": ccat, "dvec": ssm_D.reshape(1, SSM_W),
        "w_glu": w_glu.astype(BF16), "b_glu": b_glu[None, :],
        "ga": attn_perm(attn_out_g)[None, :], "gs": ssm_out_g[None, :], "w_out": w_out_p,
        "g2": norm2_g[None, :], "w_up": w_up.astype(BF16), "w_down": w_down.astype(BF16),
    }


def _state_to_lanes(re, im):
    nb = re.shape[0]
    return jnp.stack([re.reshape(nb, 2, HALF_RI), im.reshape(nb, 2, HALF_RI)], axis=2).reshape(nb, STATE_W)


def _lanes_to_state(st):
    nb = st.shape[0]
    s = st.reshape(nb, 2, 2, HALF_RI)
    return s[:, :, 0].reshape(nb, SSM_G, SSM_P), s[:, :, 1].reshape(nb, SSM_G, SSM_P)


def kernel(x_prompt, x_sample, cache_swa_k, cache_swa_v, state_ssm_re, state_ssm_im, meta_tokens, norm1_g, w_in, q_norm_g, k_norm_g, sinks, ssm_A_re, ssm_A_im, ssm_log_dt, ssm_B_re, ssm_B_im, ssm_C_re, ssm_C_im, ssm_D, w_glu, b_glu, attn_out_g, ssm_out_g, w_out, norm2_g, w_up, w_down):
    assert norm1_g.shape[0] == 1, "single-layer trunk"
    nb, seq, _ = x_prompt.shape
    nd, dseq, _ = x_sample.shape
    assert dseq == CHUNK and seq % CHUNK == 0 and seq >= WINDOW and nb % BG == 0 and nd % BG == 0
    w = _prep_weights(norm1_g[0], w_in[0], q_norm_g[0], k_norm_g[0], sinks[0], ssm_A_re[0],
                      ssm_A_im[0], ssm_log_dt[0], ssm_B_re[0], ssm_B_im[0], ssm_C_re[0],
                      ssm_C_im[0], ssm_D[0], w_glu[0], b_glu[0], attn_out_g[0], ssm_out_g[0],
                      w_out[0], norm2_g[0], w_up[0], w_down[0])

    k_meta, v_meta, st_meta = _meta_call(meta_tokens, w)
    kmeta_b = jnp.broadcast_to(k_meta[None], (nb, N_META, KV_W))
    vmeta_b = jnp.broadcast_to(v_meta[None], (nb, N_META, KV_W))
    empty_win = ((0, 0), (0, WINDOW), (0, 0))
    dp, kp, vp, stp = _mixer_call(
        x_prompt, jnp.broadcast_to(st_meta, (nb, STATE_W)), jnp.pad(kmeta_b, empty_win),
        jnp.pad(vmeta_b, empty_win), w, chunk_offset=0, n_out=WINDOW // CHUNK)
    y_prompt = _mlp_call(x_prompt.reshape(nb * seq, D_MODEL), dp.reshape(nb * seq, D_MODEL),
                         w).reshape(nb, seq, D_MODEL)

    ck = cache_swa_k[0].reshape(nd, N_META + WINDOW, KV_W)
    cv = cache_swa_v[0].reshape(nd, N_META + WINDOW, KV_W)
    ds, ks, vs, sts = _mixer_call(
        x_sample, _state_to_lanes(state_ssm_re[0], state_ssm_im[0]),
        ck, cv, w, chunk_offset=WINDOW // CHUNK, n_out=1)
    y_sample = _mlp_call(x_sample.reshape(nd * dseq, D_MODEL), ds.reshape(nd * dseq, D_MODEL),
                         w).reshape(nd, dseq, D_MODEL)

    kv5 = lambda t: t.reshape(t.shape[0], t.shape[1], N_KV_HEADS, HEAD_DIM)[None]
    new_k_prompt = kv5(jnp.concatenate([kmeta_b, kp], axis=1))
    new_v_prompt = kv5(jnp.concatenate([vmeta_b, vp], axis=1))
    srp, sip = _lanes_to_state(stp)
    srs, sis = _lanes_to_state(sts)
    return (y_prompt, y_sample, new_k_prompt, new_v_prompt, srp[None], sip[None],
            kv5(ks), kv5(vs), srs[None], sis[None])
```

```python
import functools
import math

import jax
import jax.numpy as jnp
from jax import lax
from jax.experimental import pallas as pl
from jax.experimental.pallas import tpu as pltpu

D_MODEL = 1024
CHUNK = 64
N_META = 16
HEAD_DIM = 64
ATTN_W = 512
N_KV_HEADS = 2
REP = 4
N_Q_HEADS = N_KV_HEADS * REP
KV_W = N_KV_HEADS * HEAD_DIM
WINDOW = 128
SSM_W = 512
SSM_CH = 16
SSM_G = 32
SSM_P = 64
D_FF = 4 * D_MODEL
IN_W = ATTN_W + 2 * KV_W + SSM_W
EPS = 1e-6
EIG_CLIP = -1e-4
NEG_INF = -1e30

SUBLANES = 8
LANES = 128
V7X_VMEM_BYTES = 64 * 1024 * 1024
COMPILER_VMEM_BYTES = 12 * 1024 * 1024

BG = SUBLANES
ROWS = BG * CHUNK
QK_W = ATTN_W + KV_W
V_OFF = QK_W
U_OFF = QK_W + KV_W
N_KEYS = N_META + WINDOW + CHUNK
KEY_ROWS = 256
LOG2E = 1.4426950408889634
N_SLOTS = WINDOW // CHUNK + 1
HALF_G = SSM_G // 2
HALF_RI = HALF_G * SSM_P
HALF_STATE = 2 * HALF_RI
STATE_W = 2 * HALF_STATE
HALF_U = HALF_G * SSM_CH
U_SLABS = SSM_W // LANES
SCAN_W = 512
MLP_ROWS = 1024
MLP_SUB_ROWS = 256
E_TILE = 256
MXU_N = 256
FF_BLK = 1024

BF16 = jnp.bfloat16
F32 = jnp.float32

_STEP_ORDER = (
    "unpack norm GLU_OUT AN BU:0 S:0 BU:1 SCAN:0 S:1 OP:0 PV:0 S:2 OP:1 SCAN:1 PV:1 S:3 OP:2 PV:2 S:4 OP:3 "
    "PV:3 S:5 W:0 PV:4 S:6 W:1 PV:5 QKN:0 S:7 W:2 W:3 QKN:1 PV:6 C:0 QKN:2 PV:7 C:1 W:4 GLU_IN")


def _rms(x, g):
    ms = jnp.mean(x * x, axis=-1, keepdims=True)
    return x * lax.rsqrt(ms + EPS) * g


def _project(x, g1_ref, win_ref, e_ref, gqk_ref):
    xn = _rms(x, g1_ref[...]).astype(BF16)
    z = jnp.dot(xn, win_ref[...], preferred_element_type=F32)
    zqk = z[:, :QK_W]
    sq = (zqk * zqk).astype(BF16)
    ms = jnp.concatenate(
        [jnp.dot(sq[:, o:o + n], e_ref[:n, :n], preferred_element_type=F32)
         for o, n in ((0, E_TILE), (E_TILE, E_TILE), (2 * E_TILE, QK_W - 2 * E_TILE))], axis=1)
    qk = zqk * lax.rsqrt(ms + EPS) * gqk_ref[...]
    return qk, z


def _meta_kernel(meta_ref, g1_ref, win_ref, e_ref, gqk_ref, a_ref, bbd_ref,
                 k_ref, v_ref, st_ref, bu_scr):
    qk, z = _project(meta_ref[...], g1_ref, win_ref, e_ref, gqk_ref)
    k_ref[...] = qk[:, ATTN_W:]
    v_ref[...] = z[:, V_OFF:U_OFF]
    u = z[:, U_OFF:].astype(BF16)
    for h in range(2):
        bu_scr[:, h * HALF_STATE:(h + 1) * HALF_STATE] = jnp.dot(
            u[:, h * HALF_U:(h + 1) * HALF_U], bbd_ref[h], preferred_element_type=F32)
    for h in range(2):
        ar = a_ref[2 * h:2 * h + 1, :]
        ai = a_ref[2 * h + 1:2 * h + 2, :]
        cr = h * HALF_STATE
        ci = cr + HALF_RI
        xr = jnp.zeros((1, HALF_RI), F32)
        xi = jnp.zeros((1, HALF_RI), F32)
        for t in range(N_META):
            br = bu_scr[t:t + 1, cr:cr + HALF_RI]
            bi = bu_scr[t:t + 1, ci:ci + HALF_RI]
            xr, xi = ar * xr - ai * xi + br, ar * xi + ai * xr + bi
        st_ref[:, cr:cr + HALF_RI] = xr
        st_ref[:, ci:ci + HALF_RI] = xi


def _mixer_kernel(x_ref, st0_ref, k0_ref, v0_ref,
                  g1_ref, win_ref, e_ref, gqk_ref, fill_ref, row0_ref, a_ref, bbd_ref, ccat_ref, dvec_ref,
                  wglu_ref, bglu_ref, ga_ref, gs_ref, wout_ref,
                  d_ref, kout_ref, vout_ref, st_ref,
                  z_scr, xn_scr, qs_scr, du_scr, attn_scr, utb_scr, bu_scr, ytb_scr, ybt_scr, kwin, vwin,
                  merged_scr, *, n_chunks, n_bg, chunk_offset, n_out):
    s = pl.program_id(0)
    n_steps = n_bg * n_chunks
    c = _mix_chunk(s, n_steps) % n_chunks
    live = jnp.logical_and(s >= 1, s <= n_steps)
    gc = c + chunk_offset
    row0 = row0_ref[0]

    @pl.when(s == 0)
    def _():
        z_scr[...] = jnp.zeros((ROWS, IN_W), F32)
        ybt_scr[...] = jnp.zeros((ROWS, SSM_W), F32)
        attn_scr[...] = jnp.zeros((ROWS, ATTN_W), F32)

    @pl.when(jnp.logical_and(c == 0, s <= n_steps))
    def _():
        st_ref[...] = st0_ref[...]
        kwin[:, 0:N_META + WINDOW, :] = k0_ref[...].astype(BF16)
        vwin[:, 0:N_META + WINDOW, :] = v0_ref[...].astype(BF16)
        kwin[:, N_META + WINDOW:, :] = jnp.zeros((BG, KEY_ROWS - N_META - WINDOW, KV_W), BF16)
        vwin[:, N_META + WINDOW:, :] = jnp.zeros((BG, KEY_ROWS - N_META - WINDOW, KV_W), BF16)

    lane = lax.broadcasted_iota(jnp.int32, (CHUNK, LANES), 1)
    head_lanes = (lane < HEAD_DIM, lane >= HEAD_DIM)
    key_limit = jnp.where(gc >= N_SLOTS - 1, N_KEYS, N_META + CHUNK * (gc + 1))
    key_ok = lax.broadcasted_iota(jnp.int32, (1, KEY_ROWS), 1) < key_limit

    def unpack():
        k = z_scr[:, ATTN_W:QK_W].reshape(BG, CHUNK, KV_W)
        v = z_scr[:, V_OFF:U_OFF].reshape(BG, CHUNK, KV_W)
        slot_row = pl.multiple_of(N_META + lax.rem(gc, N_SLOTS) * CHUNK, 16)
        kwin[:, pl.ds(slot_row, CHUNK), :] = k.astype(BF16)
        vwin[:, pl.ds(slot_row, CHUNK), :] = v.astype(BF16)
        out_row = pl.multiple_of(lax.rem(c, n_out) * CHUNK, CHUNK)
        kout_ref[:, pl.ds(out_row, CHUNK), :] = k
        vout_ref[:, pl.ds(out_row, CHUNK), :] = v
        for b in range(BG):
            qb = z_scr[b * CHUNK:(b + 1) * CHUNK, 0:ATTN_W]
            parts = [jnp.where(head_lanes[kvh], qb[:, r * LANES:(r + 1) * LANES], 0.0)
                     for kvh in range(N_KV_HEADS) for r in range(REP)]
            qs_scr[b] = jnp.concatenate(parts, axis=0).astype(BF16)
        for b in range(BG):
            rows = slice(b * CHUNK, (b + 1) * CHUNK)
            for j in range(U_SLABS):
                ub = z_scr[rows, U_OFF + j * LANES:U_OFF + (j + 1) * LANES]
                utb_scr[j, pl.ds(b, CHUNK, stride=BG), :] = ub
                du_scr[rows, j * LANES:(j + 1) * LANES] = dvec_ref[:, j * LANES:(j + 1) * LANES] * ub

    def out_proj(j):
        cols = slice(j * MXU_N, (j + 1) * MXU_N)
        d_ref[:, :, cols] = jnp.dot(merged_scr[...], wout_ref[:, cols],
                                    preferred_element_type=F32).reshape(BG, CHUNK, MXU_N)

    def proj_norm():
        xn_scr[...] = _rms(x_ref[...].reshape(ROWS, D_MODEL), g1_ref[...]).astype(BF16)

    def proj(j):
        cols = slice(j * MXU_N, (j + 1) * MXU_N)
        z_scr[:, cols] = jnp.dot(xn_scr[...], win_ref[:, cols], preferred_element_type=F32)

    def proj_qknorm(j):
        o = j * E_TILE
        n = min(E_TILE, QK_W - o)
        zq = z_scr[:, o:o + n]
        ms = jnp.dot((zq * zq).astype(BF16), e_ref[:n, :n], preferred_element_type=F32)
        z_scr[:, o:o + n] = zq * lax.rsqrt(ms + EPS) * gqk_ref[:, o:o + n]

    probs = {}

    def attn_scores(b):
        sc = lax.dot_general(qs_scr[b], kwin[b], (((1,), (1,)), ((), ())), preferred_element_type=F32)
        p_parts = []
        inv_l = []
        for i in range(N_Q_HEADS):
            sb = jnp.where(key_ok, sc[i * CHUNK:(i + 1) * CHUNK], fill_ref[i:i + 1, :])
            m = jnp.max(sb, axis=-1, keepdims=True)
            p = jnp.exp2(sb - m)
            l = jnp.sum(p, axis=-1, keepdims=True)
            p_parts.append(p.astype(BF16))
            inv_l.append(1.0 / l)
        probs[b] = (jnp.concatenate(p_parts, axis=0), inv_l)

    def attn_values(b):
        p_all, inv_l = probs.pop(b)
        o = jnp.dot(p_all, vwin[b], preferred_element_type=F32)
        outs = []
        for r in range(REP):
            o0 = o[r * CHUNK:(r + 1) * CHUNK] * inv_l[r]
            o1 = o[(REP + r) * CHUNK:(REP + r + 1) * CHUNK] * inv_l[REP + r]
            outs.append(jnp.where(head_lanes[0], o0, o1))
        attn_scr[b * CHUNK:(b + 1) * CHUNK, :] = jnp.concatenate(outs, axis=1)

    def s5_in(h):
        uh = jnp.concatenate([utb_scr[2 * h], utb_scr[2 * h + 1]], axis=1).astype(BF16)
        bu_scr[:, h * HALF_STATE:(h + 1) * HALF_STATE] = jnp.dot(
            uh, bbd_ref[h], preferred_element_type=F32)

    def s5_scan(h):
        for w in range(HALF_RI // SCAN_W):
            cr = h * HALF_STATE + w * SCAN_W
            ci = cr + HALF_RI
            ar = jnp.broadcast_to(a_ref[2 * h:2 * h + 1, w * SCAN_W:(w + 1) * SCAN_W], (BG, SCAN_W))
            ai = jnp.broadcast_to(a_ref[2 * h + 1:2 * h + 2, w * SCAN_W:(w + 1) * SCAN_W], (BG, SCAN_W))

            def scan_body(t, carry, cr=cr, ci=ci, ar=ar, ai=ai):
                xr, xi = carry
                r = pl.multiple_of(t * BG, BG)
                rl = pl.multiple_of(row0 + t * BG, BG)
                br = bu_scr[pl.ds(rl, BG), cr:cr + SCAN_W]
                bi = bu_scr[pl.ds(rl, BG), ci:ci + SCAN_W]
                nr = ar * xr - ai * xi + br
                ni = ar * xi + ai * xr + bi
                bu_scr[pl.ds(r, BG), cr:cr + SCAN_W] = nr
                bu_scr[pl.ds(r, BG), ci:ci + SCAN_W] = ni
                return nr, ni

            xr0 = st_ref[:, cr:cr + SCAN_W]
            xi0 = st_ref[:, ci:ci + SCAN_W]
            xr, xi = lax.fori_loop(0, CHUNK, scan_body, (xr0, xi0), unroll=True)
            st_ref[:, cr:cr + SCAN_W] = jnp.where(live, xr, xr0)
            st_ref[:, ci:ci + SCAN_W] = jnp.where(live, xi, xi0)

    def s5_out(h):
        xs = bu_scr[pl.ds(pl.multiple_of(row0, ROWS), ROWS), h * HALF_STATE:(h + 1) * HALF_STATE]
        yh = jnp.dot(xs.astype(BF16), ccat_ref[h],
                     preferred_element_type=F32)
        for jj in range(2):
            ytb_scr[2 * h + jj] = yh[:, jj * LANES:(jj + 1) * LANES]

    def glu_in():
        for b in range(BG):
            for j in range(U_SLABS):
                ybt_scr[b * CHUNK:(b + 1) * CHUNK, j * LANES:(j + 1) * LANES] = (
                    ytb_scr[j, pl.ds(b, CHUNK, stride=BG), :])
        ybt_scr[...] = jax.nn.gelu(ybt_scr[...] + du_scr[...])

    def glu_out():
        zs = ybt_scr[...]
        gate = jnp.dot(zs.astype(BF16), wglu_ref[...], preferred_element_type=F32) + bglu_ref[...]
        s_out = zs * jax.nn.sigmoid(gate)
        merged_scr[:, ATTN_W:] = _rms(s_out, gs_ref[...]).astype(BF16)

    def attn_norm():
        merged_scr[:, :ATTN_W] = _rms(attn_scr[...], ga_ref[...]).astype(BF16)

    pieces = {"unpack": unpack, "norm": proj_norm, "W": proj, "QKN": proj_qknorm, "OP": out_proj,
              "S": attn_scores, "PV": attn_values, "BU": s5_in, "SCAN": s5_scan, "C": s5_out,
              "GLU_IN": glu_in, "GLU_OUT": glu_out, "AN": attn_norm}
    for item in _STEP_ORDER.split():
        name, _, arg = item.partition(":")
        pieces[name](*([int(arg)] if arg else []))


def _mlp_kernel(x_ref, d_ref, g2_ref, wup_ref, wdown_ref, o_ref):
    for r in range(0, MLP_ROWS, MLP_SUB_ROWS):
        rows = slice(r, r + MLP_SUB_ROWS)
        h = x_ref[rows, :] + d_ref[rows, :]
        xn = _rms(h, g2_ref[...]).astype(BF16)
        acc = h
        for j in range(D_FF // FF_BLK):
            f = jnp.dot(xn, wup_ref[:, j * FF_BLK:(j + 1) * FF_BLK], preferred_element_type=F32)
            f = jnp.maximum(f, 0.0)
            acc = acc + jnp.dot((f * f).astype(BF16), wdown_ref[j * FF_BLK:(j + 1) * FF_BLK, :],
                                preferred_element_type=F32)
        o_ref[rows, :] = acc


def _mix_chunk(s, n_steps):
    return jnp.clip(s - 1, 0, n_steps - 1)


def _const_spec(shape):
    n = len(shape)
    return pl.BlockSpec(shape, lambda *_: (0,) * n, pipeline_mode=pl.Buffered(1))


def _vmem_limit(blocks, scratch=()):
    need = COMPILER_VMEM_BYTES
    need += sum(n_buf * math.prod(shape) * jnp.dtype(dtype).itemsize for shape, dtype, n_buf in blocks)
    need += sum(math.prod(m.shape) * jnp.dtype(m.dtype).itemsize for m in scratch)
    assert need <= V7X_VMEM_BYTES, need
    return need


def _spec_blocks(specs, arrays):
    return [(spec.block_shape, a.dtype, spec.pipeline_mode.buffer_count if spec.pipeline_mode else 2)
            for spec, a in zip(specs, arrays) if spec.block_shape is not None]


def _meta_call(meta, w):
    args = (meta, w["g1"], w["w_in"], w["e"], w["gqk"], w["a"], w["bbd"])
    out_shape = (jax.ShapeDtypeStruct((N_META, KV_W), F32),
                 jax.ShapeDtypeStruct((N_META, KV_W), F32),
                 jax.ShapeDtypeStruct((1, STATE_W), F32))
    scratch = [pltpu.VMEM((N_META, STATE_W), F32)]
    whole = [(a.shape, a.dtype, 1) for a in args + out_shape]
    return pl.pallas_call(
        _meta_kernel, out_shape=out_shape, scratch_shapes=scratch,
        compiler_params=pltpu.CompilerParams(vmem_limit_bytes=_vmem_limit(whole, scratch)),
        name="meta",
    )(*args)


def _mixer_call(x, st0, k0, v0, w, *, chunk_offset, n_out):
    nb, s_len, _ = x.shape
    n_chunks = s_len // CHUNK
    n_bg = nb // BG
    n_steps = n_bg * n_chunks
    assert n_chunks % n_out == 0
    grid = (n_steps + 2,)

    def chunk_block(gidx):
        return (gidx // n_chunks, gidx % n_chunks, 0)

    x_index = lambda s: chunk_block(jnp.minimum(s, n_steps - 1))
    d_index = lambda s: chunk_block(jnp.clip(s - 2, 0, n_steps - 1))
    per_bg = lambda s: (_mix_chunk(s, n_steps) // n_chunks, 0, 0)
    per_bg2 = lambda s: (_mix_chunk(s, n_steps) // n_chunks, 0)
    in_specs = [
        pl.BlockSpec((BG, CHUNK, D_MODEL), x_index),
        pl.BlockSpec((BG, STATE_W), per_bg2),
        pl.BlockSpec((BG, N_META + WINDOW, KV_W), per_bg),
        pl.BlockSpec((BG, N_META + WINDOW, KV_W), per_bg),
        _const_spec((1, D_MODEL)),
        _const_spec((D_MODEL, IN_W)),
        _const_spec((E_TILE, E_TILE)),
        _const_spec((1, QK_W)),
        _const_spec((N_Q_HEADS, KEY_ROWS)),
        pl.BlockSpec(memory_space=pltpu.SMEM),
        _const_spec((4, HALF_RI)),
        _const_spec((2, HALF_U, HALF_STATE)),
        _const_spec((2, HALF_STATE, HALF_U)),
        _const_spec((1, SSM_W)),
        _const_spec((SSM_W, SSM_W)),
        _const_spec((1, SSM_W)),
        _const_spec((1, ATTN_W)),
        _const_spec((1, SSM_W)),
        _const_spec((D_MODEL, D_MODEL)),
    ]
    out_specs = (
        pl.BlockSpec((BG, CHUNK, D_MODEL), d_index),
        pl.BlockSpec((BG, n_out * CHUNK, KV_W), per_bg),
        pl.BlockSpec((BG, n_out * CHUNK, KV_W), per_bg),
        pl.BlockSpec((BG, STATE_W), per_bg2),
    )
    out_shape = (
        jax.ShapeDtypeStruct((nb, s_len, D_MODEL), F32),
        jax.ShapeDtypeStruct((nb, n_out * CHUNK, KV_W), F32),
        jax.ShapeDtypeStruct((nb, n_out * CHUNK, KV_W), F32),
        jax.ShapeDtypeStruct((nb, STATE_W), F32),
    )
    scratch = [
        pltpu.VMEM((ROWS, IN_W), F32),
        pltpu.VMEM((ROWS, D_MODEL), BF16),
        pltpu.VMEM((BG, N_Q_HEADS * CHUNK, LANES), BF16),
        pltpu.VMEM((ROWS, SSM_W), F32),
        pltpu.VMEM((ROWS, ATTN_W), F32),
        pltpu.VMEM((U_SLABS, ROWS, LANES), F32),
        pltpu.VMEM((ROWS, STATE_W), F32),
        pltpu.VMEM((U_SLABS, ROWS, LANES), F32),
        pltpu.VMEM((ROWS, SSM_W), F32),
        pltpu.VMEM((BG, KEY_ROWS, KV_W), BF16),
        pltpu.VMEM((BG, KEY_ROWS, KV_W), BF16),
        pltpu.VMEM((ROWS, D_MODEL), BF16),
    ]
    kern = functools.partial(_mixer_kernel, n_chunks=n_chunks, n_bg=n_bg,
                             chunk_offset=chunk_offset, n_out=n_out)
    args = (x, st0, k0, v0,
            w["g1"], w["w_in"], w["e"], w["gqk"], w["fill"], jnp.zeros((1,), jnp.int32),
            w["a"], w["bbd"], w["ccat"], w["dvec"],
            w["w_glu"], w["b_glu"], w["ga"], w["gs"], w["w_out"])
    blocks = _spec_blocks(in_specs, args) + _spec_blocks(out_specs, out_shape)
    return pl.pallas_call(
        kern, grid=grid, in_specs=in_specs, out_specs=out_specs, out_shape=out_shape,
        scratch_shapes=scratch,
        compiler_params=pltpu.CompilerParams(
            dimension_semantics=("arbitrary",), vmem_limit_bytes=_vmem_limit(blocks, scratch)),
        name="mixer",
    )(*args)


def _mlp_call(x, d, w):
    n = x.shape[0]
    rows = pl.BlockSpec((MLP_ROWS, D_MODEL), lambda i: (i, 0))
    in_specs = [rows, rows, _const_spec((1, D_MODEL)), _const_spec((D_MODEL, D_FF)),
                _const_spec((D_FF, D_MODEL))]
    args = (x, d, w["g2"], w["w_up"], w["w_down"])
    out_shape = jax.ShapeDtypeStruct((n, D_MODEL), F32)
    blocks = _spec_blocks(in_specs, args) + _spec_blocks([rows], [out_shape])
    return pl.pallas_call(
        _mlp_kernel, grid=(n // MLP_ROWS,), in_specs=in_specs, out_specs=rows, out_shape=out_shape,
        compiler_params=pltpu.CompilerParams(
            dimension_semantics=("arbitrary",), vmem_limit_bytes=_vmem_limit(blocks)),
        name="mlp",
    )(*args)


def _prep_weights(norm1_g, w_in, q_norm_g, k_norm_g, sinks, ssm_A_re, ssm_A_im, ssm_log_dt,
                  ssm_B_re, ssm_B_im, ssm_C_re, ssm_C_im, ssm_D, w_glu, b_glu, attn_out_g,
                  ssm_out_g, w_out, norm2_g, w_up, w_down):
    def attn_perm(m):
        lead = m.shape[:-1]
        return m.reshape(lead + (N_KV_HEADS, REP, HEAD_DIM)).swapaxes(-3, -2).reshape(lead + (ATTN_W,))

    w_in_p = jnp.concatenate([attn_perm(w_in[:, :ATTN_W]), w_in[:, ATTN_W:]], axis=1).astype(BF16)
    e = jnp.kron(jnp.eye(E_TILE // HEAD_DIM, dtype=F32),
                 jnp.full((HEAD_DIM, HEAD_DIM), 1.0 / HEAD_DIM, F32)).astype(BF16)
    gqk = jnp.concatenate([jnp.tile(q_norm_g, N_Q_HEADS) * (HEAD_DIM ** -0.5 * LOG2E),
                           jnp.tile(k_norm_g, N_KV_HEADS)])[None, :]

    fill = jnp.full((N_Q_HEADS, KEY_ROWS), NEG_INF, F32).at[:, N_KEYS].set(sinks * LOG2E)

    lam_re = jnp.minimum(ssm_A_re, EIG_CLIP)
    lam_im = ssm_A_im
    dt = jnp.exp(ssm_log_dt)[:, None]
    mag = jnp.exp(lam_re * dt)
    ar = mag * jnp.cos(lam_im * dt)
    ai = mag * jnp.sin(lam_im * dt)
    den = lam_re * lam_re + lam_im * lam_im
    cr = ((ar - 1.0) * lam_re + ai * lam_im) / den
    ci = (ai * lam_re - (ar - 1.0) * lam_im) / den
    bb_re = cr[..., None] * ssm_B_re - ci[..., None] * ssm_B_im
    bb_im = cr[..., None] * ssm_B_im + ci[..., None] * ssm_B_re
    a = jnp.stack([ar.reshape(2, HALF_RI), ai.reshape(2, HALF_RI)], axis=1).reshape(4, HALF_RI)
    eye = jnp.eye(HALF_G, dtype=F32)
    bb = jnp.stack([bb_re, bb_im], 0).reshape(2, 2, HALF_G, SSM_P, SSM_CH)
    bbd = jnp.einsum('ab,rhapc->hacrbp', eye, bb).reshape(2, HALF_U, HALF_STATE).astype(BF16)
    cc = jnp.stack([ssm_C_re, -ssm_C_im], 0).reshape(2, 2, HALF_G, SSM_CH, SSM_P)
    ccat = jnp.einsum('ab,rhacp->hrapbc', eye, cc).reshape(2, HALF_STATE, HALF_U).astype(BF16)

    w_out_p = jnp.concatenate(
        [attn_perm(w_out[:ATTN_W].T).T, w_out[ATTN_W:]], axis=0).astype(BF16)
    return {
        "g1": norm1_g[None, :], "w_in": w_in_p, "e": e, "gqk": gqk, "fill": fill,
        "a": a, "bbd": bbd, "ccat": ccat, "dvec": ssm_D.reshape(1, SSM_W),
        "w_glu": w_glu.astype(BF16), "b_glu": b_glu[None, :],
        "ga": attn_perm(attn_out_g)[None, :], "gs": ssm_out_g[None, :], "w_out": w_out_p,
        "g2": norm2_g[None, :], "w_up": w_up.astype(BF16), "w_down": w_down.astype(BF16),
    }


def _state_to_lanes(re, im):
    nb = re.shape[0]
    return jnp.stack([re.reshape(nb, 2, HALF_RI), im.reshape(nb, 2, HALF_RI)], axis=2).reshape(nb, STATE_W)


def _lanes_to_state(st):
    nb = st.shape[0]
    s = st.reshape(nb, 2, 2, HALF_RI)
    return s[:, :, 0].reshape(nb, SSM_G, SSM_P), s[:, :, 1].reshape(nb, SSM_G, SSM_P)


def kernel(x_prompt, x_sample, cache_swa_k, cache_swa_v, state_ssm_re, state_ssm_im, meta_tokens, norm1_g, w_in, q_norm_g, k_norm_g, sinks, ssm_A_re, ssm_A_im, ssm_log_dt, ssm_B_re, ssm_B_im, ssm_C_re, ssm_C_im, ssm_D, w_glu, b_glu, attn_out_g, ssm_out_g, w_out, norm2_g, w_up, w_down):
    assert norm1_g.shape[0] == 1, "single-layer trunk"
    nb, seq, _ = x_prompt.shape
    nd, dseq, _ = x_sample.shape
    assert dseq == CHUNK and seq % CHUNK == 0 and seq >= WINDOW and nb % BG == 0 and nd % BG == 0
    w = _prep_weights(norm1_g[0], w_in[0], q_norm_g[0], k_norm_g[0], sinks[0], ssm_A_re[0],
                      ssm_A_im[0], ssm_log_dt[0], ssm_B_re[0], ssm_B_im[0], ssm_C_re[0],
                      ssm_C_im[0], ssm_D[0], w_glu[0], b_glu[0], attn_out_g[0], ssm_out_g[0],
                      w_out[0], norm2_g[0], w_up[0], w_down[0])

    k_meta, v_meta, st_meta = _meta_call(meta_tokens, w)
    kmeta_b = jnp.broadcast_to(k_meta[None], (nb, N_META, KV_W))
    vmeta_b = jnp.broadcast_to(v_meta[None], (nb, N_META, KV_W))
    empty_win = ((0, 0), (0, WINDOW), (0, 0))
    dp, kp, vp, stp = _mixer_call(
        x_prompt, jnp.broadcast_to(st_meta, (nb, STATE_W)), jnp.pad(kmeta_b, empty_win),
        jnp.pad(vmeta_b, empty_win), w, chunk_offset=0, n_out=WINDOW // CHUNK)
    y_prompt = _mlp_call(x_prompt.reshape(nb * seq, D_MODEL), dp.reshape(nb * seq, D_MODEL),
                         w).reshape(nb, seq, D_MODEL)

    ck = cache_swa_k[0].reshape(nd, N_META + WINDOW, KV_W)
    cv = cache_swa_v[0].reshape(nd, N_META + WINDOW, KV_W)
    ds, ks, vs, sts = _mixer_call(
        x_sample, _state_to_lanes(state_ssm_re[0], state_ssm_im[0]),
        ck, cv, w, chunk_offset=WINDOW // CHUNK, n_out=1)
    y_sample = _mlp_call(x_sample.reshape(nd * dseq, D_MODEL), ds.reshape(nd * dseq, D_MODEL),
                         w).reshape(nd, dseq, D_MODEL)

    kv5 = lambda t: t.reshape(t.shape[0], t.shape[1], N_KV_HEADS, HEAD_DIM)[None]
    new_k_prompt = kv5(jnp.concatenate([kmeta_b, kp], axis=1))
    new_v_prompt = kv5(jnp.concatenate([vmeta_b, vp], axis=1))
    srp, sip = _lanes_to_state(stp)
    srs, sis = _lanes_to_state(sts)
    return (y_prompt, y_sample, new_k_prompt, new_v_prompt, srp[None], sip[None],
            kv5(ks), kv5(vs), srs[None], sis[None])
```
